```python
import math
import jax, jax.numpy as jnp
from jax import lax
import numpy as np

D_MODEL = 1024
BATCH = 2
SEQ = 8192
DEPTH = 4
DEC_BATCH = 128
DEC_SEQ = 8
PAST_LEN = 8192
PAGE_SIZE = 128

N_META = 16
N_A_LAYERS = DEPTH // 2
N_B_LAYERS = DEPTH - N_A_LAYERS
A_DK = 128
A_DV = 128
A_HEADS = D_MODEL // A_DV
CONV_W = 4
CHUNK = 64
CONV_DIM = A_HEADS * (2 * A_DK + A_DV)
A_PROJ = CONV_DIM + A_HEADS * A_DV + 2 * A_HEADS
B_HD = 64
B_HEADS = D_MODEL // B_HD
B_KV_HEADS = B_HEADS // 4
B_GROUP = B_HEADS // B_KV_HEADS
WINDOW = 128
BLOCK = 128
D_FF = ((8 * D_MODEL // 3 + 127) // 128) * 128
N_EXPERTS = 8
TOP_K = 2
D_EXPERT = D_MODEL
N_DENSE = (DEPTH + 1) // 2
N_MOE = DEPTH // 2
DEEPNORM_ALPHA = (2 * DEPTH) ** 0.25
DEEPNORM_BETA = (8 * DEPTH) ** -0.25
LN_EPS = 1e-5
RMS_EPS = 1e-6
NEG_INF = -1e30
F32 = jnp.float32

kernel_name = 'yoco_gdn_swa_hybrid_step'


def _layer_norm(x, g, b):
    xf = x.astype(F32)
    mu = jnp.mean(xf, -1, keepdims=True)
    xc = xf - mu
    var = jnp.mean(xc * xc, -1, keepdims=True)
    return (xc * lax.rsqrt(var + LN_EPS) * g.astype(F32) + b.astype(F32)).astype(x.dtype)


def _l2norm(t):
    return t * lax.rsqrt(jnp.sum(t * t, -1, keepdims=True) + 1e-6)


def _alibi_slopes():
    return 2.0 ** (-8.0 * jnp.arange(1, B_HEADS + 1, dtype=F32) / B_HEADS)


def _gated_delta_chunked(q, k, v, g, beta, s0, chunk):
    bsz, nh, L, dk = q.shape
    dv = v.shape[-1]
    n = L // chunk
    blk = lambda t: t.reshape(bsz, nh, n, chunk, *t.shape[3:])
    q, k, v, g, beta = blk(q), blk(k), blk(v), blk(g), blk(beta)
    G = jnp.cumsum(g, axis=-1)
    incl = jnp.tril(jnp.ones((chunk, chunk), bool))
    strict = jnp.tril(jnp.ones((chunk, chunk), bool), -1)
    diff = G[..., :, None] - G[..., None, :]
    decay = jnp.where(incl, jnp.exp(jnp.where(incl, diff, 0.0)), 0.0)
    kb = k * beta[..., None]
    vb = v * beta[..., None]
    A = jnp.where(strict, jnp.einsum('bhnik,bhnjk->bhnij', kb, k) * decay, 0.0)
    M = A + jnp.eye(chunk, dtype=F32)
    u = lax.linalg.triangular_solve(M, vb, left_side=True, lower=True, unit_diagonal=True)
    w = lax.linalg.triangular_solve(M, kb * jnp.exp(G)[..., None], left_side=True, lower=True, unit_diagonal=True)
    qk = jnp.where(incl, jnp.einsum('bhnik,bhnjk->bhnij', q, k) * decay, 0.0)
    qg = q * jnp.exp(G)[..., None]
    kdec = k * jnp.exp(G[..., -1:] - G)[..., None]
    glast = jnp.exp(G[..., -1])

    def step(S, xs):
        u_c, w_c, qk_c, qg_c, kdec_c, gl_c = xs
        v_new = u_c - jnp.einsum('bhck,bhkv->bhcv', w_c, S)
        o = jnp.einsum('bhck,bhkv->bhcv', qg_c, S) + jnp.einsum('bhij,bhjv->bhiv', qk_c, v_new)
        S = S * gl_c[..., None, None] + jnp.einsum('bhck,bhcv->bhkv', kdec_c, v_new)
        return S, o

    xs = tuple(jnp.moveaxis(t, 2, 0) for t in (u, w, qk, qg, kdec, glast))
    s_fin, o = lax.scan(step, s0, xs)
    o = jnp.moveaxis(o, 0, 2).reshape(bsz, nh, L, dv)
    return o, s_fin


def _gdn_mixer(x, w_in, conv_w, A_log, dt_bias, norm_w, w_out, s0, conv0, lead_pad, chunk):
    bsz, L, _ = x.shape
    proj = x @ w_in
    o1 = CONV_DIM
    o2 = o1 + A_HEADS * A_DV
    o3 = o2 + A_HEADS
    qkv, z, b, a = proj[..., :o1], proj[..., o1:o2], proj[..., o2:o3], proj[..., o3:]
    xp = jnp.concatenate([conv0.astype(qkv.dtype), qkv], axis=1)
    conv = xp[:, 0:L] * conv_w[0]
    for j in range(1, CONV_W):
        conv = conv + xp[:, j:j + L] * conv_w[j]
    conv = jax.nn.silu(conv)
    new_conv = xp[:, L:]

    def heads(t, d):
        return jnp.swapaxes(t.reshape(bsz, L, A_HEADS, d), 1, 2).astype(F32)

    q = _l2norm(heads(conv[..., :A_HEADS * A_DK], A_DK)) * (A_DK ** -0.5)
    k = _l2norm(heads(conv[..., A_HEADS * A_DK:2 * A_HEADS * A_DK], A_DK))
    v = heads(conv[..., 2 * A_HEADS * A_DK:], A_DV)
    beta = jnp.swapaxes(jax.nn.sigmoid(b.astype(F32)), 1, 2)
    g = -jnp.exp(A_log.astype(F32)) * jax.nn.softplus(a.astype(F32) + dt_bias.astype(F32))
    g = jnp.swapaxes(g, 1, 2)
    if lead_pad:
        p4 = ((0, 0), (0, 0), (lead_pad, 0), (0, 0))
        p3 = ((0, 0), (0, 0), (lead_pad, 0))
        q, k, v = jnp.pad(q, p4), jnp.pad(k, p4), jnp.pad(v, p4)
        g, beta = jnp.pad(g, p3), jnp.pad(beta, p3)
    o, s_new = _gated_delta_chunked(q, k, v, g, beta, s0.astype(F32), chunk)
    o = jnp.swapaxes(o[:, :, lead_pad:], 1, 2)
    o = o * lax.rsqrt(jnp.mean(o * o, -1, keepdims=True) + RMS_EPS) * norm_w.astype(F32)
    o = o * jax.nn.silu(z.astype(F32).reshape(bsz, L, A_HEADS, A_DV))
    out = o.reshape(bsz, L, A_HEADS * A_DV).astype(x.dtype) @ w_out
    return out, s_new, new_conv


def _sink_softmax(s, sinks):
    sk = sinks.astype(F32).reshape(B_KV_HEADS, B_GROUP, 1, 1)
    sk = jnp.broadcast_to(sk, s.shape[:-1] + (1,))
    p = jax.nn.softmax(jnp.concatenate([s, sk], axis=-1), axis=-1)
    return p[..., :-1]


def _window_attn_prompt(q, k, v, sinks, pad):
    bsz, L = q.shape[:2]
    pw = ((0, 0), (pad, 0), (0, 0), (0, 0))
    q, k, v = jnp.pad(q, pw), jnp.pad(k, pw), jnp.pad(v, pw)
    nb = (L + pad) // BLOCK
    qb = q.reshape(bsz, nb, BLOCK, B_KV_HEADS, B_GROUP, B_HD)

    def band(t):
        t = t.reshape(bsz, nb, BLOCK, B_KV_HEADS, B_HD)
        prev = jnp.pad(t, ((0, 0), (1, 0), (0, 0), (0, 0), (0, 0)))[:, :-1]
        return jnp.concatenate([prev, t], axis=2)

    kk, vv = band(k), band(v)
    pos = (jnp.arange(nb * BLOCK) - pad).reshape(nb, BLOCK)
    kpos = jnp.concatenate([pos - BLOCK, pos], axis=1)
    dist = pos[:, :, None] - kpos[:, None, :]
    valid = (kpos[:, None, :] >= 0) & (dist >= 0) & (dist <= WINDOW)
    slopes = _alibi_slopes().reshape(B_KV_HEADS, B_GROUP)
    s = jnp.einsum('bnqkgd,bnskd->bnkgqs', qb, kk).astype(F32) * (B_HD ** -0.5)
    s = s - slopes[:, :, None, None] * dist[:, None, None].astype(F32)
    s = jnp.where(valid[:, None, None], s, NEG_INF)
    p = _sink_softmax(s, sinks)
    o = jnp.einsum('bnkgqs,bnskd->bnqkgd', p.astype(vv.dtype), vv)
    return o.reshape(bsz, nb * BLOCK, B_HEADS * B_HD)[:, pad:]


def _window_attn_sample(q, kk, vv, sinks):
    bsz, S = q.shape[:2]
    W = kk.shape[1] - S
    qpos = W + jnp.arange(S)
    kpos = jnp.arange(W + S)
    dist = qpos[:, None] - kpos[None, :]
    valid = (dist >= 0) & (dist <= WINDOW)
    slopes = _alibi_slopes().reshape(B_KV_HEADS, B_GROUP)
    qg = q.reshape(bsz, S, B_KV_HEADS, B_GROUP, B_HD)
    s = jnp.einsum('bqkgd,bskd->bkgqs', qg, kk).astype(F32) * (B_HD ** -0.5)
    s = s - slopes[:, :, None, None] * dist.astype(F32)
    s = jnp.where(valid, s, NEG_INF)
    p = _sink_softmax(s, sinks)
    o = jnp.einsum('bkgqs,bskd->bqkgd', p.astype(vv.dtype), vv)
    return o.reshape(bsz, S, B_HEADS * B_HD)


def _dense_ffn(x, w_gu, w_down):
    gu = x @ w_gu
    return (jax.nn.silu(gu[..., :D_FF]) * gu[..., D_FF:]) @ w_down


def _moe_ffn(x, w_router, b_router, w_gu, w_down):
    logits = (x @ w_router).astype(F32) + b_router.astype(F32)
    probs = jax.nn.softmax(logits, axis=-1)
    top_p, top_i = lax.top_k(probs, TOP_K)
    top_p = top_p / jnp.sum(top_p, -1, keepdims=True)
    gates = jnp.sum(jax.nn.one_hot(top_i, N_EXPERTS, dtype=F32) * top_p[..., None], axis=-2)
    gates = gates.astype(x.dtype)
    y = jnp.zeros_like(x)
    for e in range(N_EXPERTS):
        gu = x @ w_gu[e]
        h = jax.nn.silu(gu[..., :D_EXPERT]) * gu[..., D_EXPERT:]
        y = y + gates[..., e:e + 1] * (h @ w_down[e])
    return y


def _trunk(x, gdn_s0, conv_s0, win_k0, win_v0, a_w_in, a_conv_w, a_A_log, a_dt_bias, a_norm_w, a_w_out,
           b_w_kv, b_w_q, b_sinks, b_w_o, ln_mix_g, ln_mix_b, ln_ffn_g, ln_ffn_b,
           ffd_w_gu, ffd_w_down, moe_w_router, moe_b_router, moe_w_gu, moe_w_down):
    prompt = win_k0 is None
    bsz, L, _ = x.shape
    new_s, new_c = [], []
    k_sh = v_sh = new_wk = new_wv = None
    for l in range(DEPTH):
        if l < N_A_LAYERS:
            if prompt:
                lead, chunk = (-N_META) % CHUNK, CHUNK
            else:
                lead, chunk = 0, L
            mix, s_l, c_l = _gdn_mixer(x, a_w_in[l], a_conv_w[l], a_A_log[l], a_dt_bias[l], a_norm_w[l],
                                       a_w_out[l], gdn_s0[l], conv_s0[l], lead, chunk)
            new_s.append(s_l)
            new_c.append(c_l)
        else:
            j = l - N_A_LAYERS
            if j == 0:
                kv = (x @ b_w_kv).reshape(bsz, L, 2, B_KV_HEADS, B_HD)
                k_sh, v_sh = kv[:, :, 0], kv[:, :, 1]
                if not prompt:
                    k_sh = jnp.concatenate([win_k0.astype(k_sh.dtype), k_sh], axis=1)
                    v_sh = jnp.concatenate([win_v0.astype(v_sh.dtype), v_sh], axis=1)
                new_wk, new_wv = k_sh[:, -WINDOW:], v_sh[:, -WINDOW:]
            q = (x @ b_w_q[j]).reshape(bsz, L, B_HEADS, B_HD)
            if prompt:
                o = _window_attn_prompt(q, k_sh, v_sh, b_sinks[j], (-N_META) % BLOCK)
            else:
                o = _window_attn_sample(q, k_sh, v_sh, b_sinks[j])
            mix = o @ b_w_o[j]
        x = _layer_norm(DEEPNORM_ALPHA * x + mix, ln_mix_g[l], ln_mix_b[l])
        if l % 2 == 0:
            f = _dense_ffn(x, ffd_w_gu[l // 2], ffd_w_down[l // 2])
        else:
            f = _moe_ffn(x, moe_w_router[l // 2], moe_b_router[l // 2], moe_w_gu[l // 2], moe_w_down[l // 2])
        x = _layer_norm(DEEPNORM_ALPHA * x + f, ln_ffn_g[l], ln_ffn_b[l])
    return x, jnp.stack(new_s), jnp.stack(new_c), new_wk, new_wv


def setup_inputs(seed: int = 0) -> dict:
    key = jax.random.key(seed)
    ks = iter(jax.random.split(key, 40))

    def nrm(shape, scale):
        return jax.random.normal(next(ks), shape, F32) * scale

    D = D_MODEL
    win_buf = min(WINDOW, PAST_LEN)
    x_prompt = nrm((BATCH, SEQ, D), 1.0)
    x_sample = nrm((DEC_BATCH, DEC_SEQ, D), 1.0)
    state_gdn = nrm((N_A_LAYERS, DEC_BATCH, A_HEADS, A_DK, A_DV), 0.1)
    state_conv = nrm((N_A_LAYERS, DEC_BATCH, CONV_W - 1, CONV_DIM), 1.0)
    cache_win_k = nrm((DEC_BATCH, win_buf, B_KV_HEADS, B_HD), 1.0)
    cache_win_v = nrm((DEC_BATCH, win_buf, B_KV_HEADS, B_HD), 1.0)
    meta_tokens = nrm((N_META, D), 1.0)
    a_w_in = nrm((N_A_LAYERS, D, A_PROJ), D ** -0.5)
    a_conv_w = nrm((N_A_LAYERS, CONV_W, CONV_DIM), CONV_W ** -0.5)
    a_A_log = jnp.log(jax.random.uniform(next(ks), (N_A_LAYERS, A_HEADS), F32, 1.0, 16.0))
    dt = jnp.exp(jax.random.uniform(next(ks), (N_A_LAYERS, A_HEADS), F32, math.log(1e-3), math.log(1e-1)))
    a_dt_bias = dt + jnp.log(-jnp.expm1(-dt))
    a_norm_w = 1.0 + nrm((N_A_LAYERS, A_DV), 0.02)
    a_w_out = nrm((N_A_LAYERS, A_HEADS * A_DV, D), DEEPNORM_BETA * (A_HEADS * A_DV) ** -0.5)
    b_w_kv = nrm((D, 2 * B_KV_HEADS * B_HD), D ** -0.5)
    b_w_q = nrm((N_B_LAYERS, D, B_HEADS * B_HD), D ** -0.5)
    b_sinks = nrm((N_B_LAYERS, B_HEADS), 0.5)
    b_w_o = nrm((N_B_LAYERS, B_HEADS * B_HD, D), DEEPNORM_BETA * (B_HEADS * B_HD) ** -0.5)
    ln_mix_g = 1.0 + nrm((DEPTH, D), 0.02)
    ln_mix_b = nrm((DEPTH, D), 0.02)
    ln_ffn_g = 1.0 + nrm((DEPTH, D), 0.02)
    ln_ffn_b = nrm((DEPTH, D), 0.02)
    ffd_w_gu = nrm((N_DENSE, D, 2 * D_FF), D ** -0.5)
    ffd_w_down = nrm((N_DENSE, D_FF, D), DEEPNORM_BETA * D_FF ** -0.5)
    moe_w_router = nrm((N_MOE, D, N_EXPERTS), D ** -0.5)
    moe_b_router = nrm((N_MOE, N_EXPERTS), 0.01)
    moe_w_gu = nrm((N_MOE, N_EXPERTS, D, 2 * D_EXPERT), D ** -0.5)
    moe_w_down = nrm((N_MOE, N_EXPERTS, D_EXPERT, D), DEEPNORM_BETA * D_EXPERT ** -0.5)
    return {'x_prompt': x_prompt, 'x_sample': x_sample, 'state_gdn': state_gdn, 'state_conv': state_conv,
            'cache_win_k': cache_win_k, 'cache_win_v': cache_win_v, 'meta_tokens': meta_tokens,
            'a_w_in': a_w_in, 'a_conv_w': a_conv_w, 'a_A_log': a_A_log, 'a_dt_bias': a_dt_bias,
            'a_norm_w': a_norm_w, 'a_w_out': a_w_out, 'b_w_kv': b_w_kv, 'b_w_q': b_w_q, 'b_sinks': b_sinks,
            'b_w_o': b_w_o, 'ln_mix_g': ln_mix_g, 'ln_mix_b': ln_mix_b, 'ln_ffn_g': ln_ffn_g, 'ln_ffn_b': ln_ffn_b,
            'ffd_w_gu': ffd_w_gu, 'ffd_w_down': ffd_w_down, 'moe_w_router': moe_w_router,
            'moe_b_router': moe_b_router, 'moe_w_gu': moe_w_gu, 'moe_w_down': moe_w_down}


def reference(x_prompt, x_sample, state_gdn, state_conv, cache_win_k, cache_win_v, meta_tokens,
              a_w_in, a_conv_w, a_A_log, a_dt_bias, a_norm_w, a_w_out, b_w_kv, b_w_q, b_sinks, b_w_o,
              ln_mix_g, ln_mix_b, ln_ffn_g, ln_ffn_b, ffd_w_gu, ffd_w_down,
              moe_w_router, moe_b_router, moe_w_gu, moe_w_down):
    bsz = x_prompt.shape[0]
    meta = jnp.broadcast_to(meta_tokens.astype(x_prompt.dtype)[None], (bsz, N_META, D_MODEL))
    xp = jnp.concatenate([meta, x_prompt], axis=1)
    zero_s = jnp.zeros((N_A_LAYERS, bsz, A_HEADS, A_DK, A_DV), F32)
    zero_c = jnp.zeros((N_A_LAYERS, bsz, CONV_W - 1, CONV_DIM), x_prompt.dtype)
    yp, p_gdn, p_conv, p_wk, p_wv = _trunk(
        xp, zero_s, zero_c, None, None, a_w_in, a_conv_w, a_A_log, a_dt_bias, a_norm_w, a_w_out,
        b_w_kv, b_w_q, b_sinks, b_w_o, ln_mix_g, ln_mix_b, ln_ffn_g, ln_ffn_b,
        ffd_w_gu, ffd_w_down, moe_w_router, moe_b_router, moe_w_gu, moe_w_down)
    y_prompt = yp[:, N_META:]
    y_sample, s_gdn, s_conv, s_wk, s_wv = _trunk(
        x_sample, state_gdn, state_conv, cache_win_k, cache_win_v, a_w_in, a_conv_w, a_A_log, a_dt_bias,
        a_norm_w, a_w_out, b_w_kv, b_w_q, b_sinks, b_w_o, ln_mix_g, ln_mix_b, ln_ffn_g, ln_ffn_b,
        ffd_w_gu, ffd_w_down, moe_w_router, moe_b_router, moe_w_gu, moe_w_down)
    return (y_prompt, y_sample, p_gdn, p_conv, p_wk, p_wv, s_gdn, s_conv, s_wk, s_wv)
```

```python
import functools
import math

import jax
import jax.numpy as jnp
from jax import lax
from jax.experimental import pallas as pl
from jax.experimental.pallas import tpu as pltpu

F32 = jnp.float32
BF16 = jnp.bfloat16

D_MODEL = 1024
N_META = 16
A_HEADS = 8
A_DK = 128
A_DV = 128
CONV_W = 4
CONV_DIM = A_HEADS * (2 * A_DK + A_DV)
A_MAIN = CONV_DIM + A_HEADS * A_DV
A_PROJ_PAD = A_MAIN + 128
B_HD = 64
B_HEADS = 16
B_KV_HEADS = 4
B_GROUP = 4
B_KV = B_KV_HEADS * B_HD
WINDOW = 128
N_EXPERTS = 8
DEPTH = 4
N_A_LAYERS = 2
DEEPNORM_ALPHA = (2 * DEPTH) ** 0.25
LN_EPS = 1e-5
RMS_EPS = 1e-6
NEG_INF = -1e30

ROWS = 128
CHUNK = 64
LEAD = (-N_META) % ROWS
SAMPLE_BB = CHUNK // 8
MOE_TM = 256
VMEM_LIMIT = 56 * 1024 * 1024


def _cparams(*sem):
    return pltpu.CompilerParams(dimension_semantics=sem, vmem_limit_bytes=VMEM_LIMIT)


def _bdot(a, b):
    return jnp.dot(a.astype(BF16), b.astype(BF16), preferred_element_type=F32)


def _bdot_nt(a, b):
    return lax.dot_general(a.astype(BF16), b.astype(BF16), (((1,), (1,)), ((), ())), preferred_element_type=F32)


def _bdot_tn(a, b):
    return lax.dot_general(a.astype(BF16), b.astype(BF16), (((0,), (0,)), ((), ())), preferred_element_type=F32)


def _sigmoid(x):
    return 1.0 / (1.0 + jnp.exp(-x))


def _silu(x):
    return x * _sigmoid(x)


def _softplus(x):
    return jnp.maximum(x, 0.0) + jnp.log(1.0 + jnp.exp(-jnp.abs(x)))


def _layer_norm_rows(v, g, b):
    mu = jnp.mean(v, axis=-1, keepdims=True)
    vc = v - mu
    var = jnp.mean(vc * vc, axis=-1, keepdims=True)
    return vc * lax.rsqrt(var + LN_EPS) * g + b


def _mm_kernel(x_ref, w_ref, o_ref):
    o_ref[...] = jnp.dot(x_ref[...], w_ref[...], preferred_element_type=F32).astype(o_ref.dtype)


def _matmul(x, w, out_dtype, tm, tn):
    m, k = x.shape
    n = w.shape[1]
    return pl.pallas_call(
        _mm_kernel,
        grid=(n // tn, m // tm),
        in_specs=[pl.BlockSpec((tm, k), lambda j, i: (i, 0)), pl.BlockSpec((k, tn), lambda j, i: (0, j))],
        out_specs=pl.BlockSpec((tm, tn), lambda j, i: (i, j)),
        out_shape=jax.ShapeDtypeStruct((m, n), out_dtype),
        compiler_params=_cparams("arbitrary", "arbitrary"),
        name="matmul",
    )(x, w)


def _swiglu_kernel(x_ref, wg_ref, wu_ref, o_ref):
    x = x_ref[...]
    g = jnp.dot(x, wg_ref[...], preferred_element_type=F32)
    u = jnp.dot(x, wu_ref[...], preferred_element_type=F32)
    o_ref[...] = (_silu(g) * u).astype(o_ref.dtype)


def _matmul_swiglu(x, w_gu, d_ff, tm, tn):
    m, k = x.shape
    nt = d_ff // tn
    return pl.pallas_call(
        _swiglu_kernel,
        grid=(nt, m // tm),
        in_specs=[pl.BlockSpec((tm, k), lambda j, i: (i, 0)),
                  pl.BlockSpec((k, tn), lambda j, i: (0, j)),
                  pl.BlockSpec((k, tn), lambda j, i: (0, j + nt))],
        out_specs=pl.BlockSpec((tm, tn), lambda j, i: (i, j)),
        out_shape=jax.ShapeDtypeStruct((m, d_ff), BF16),
        compiler_params=_cparams("arbitrary", "arbitrary"),
        name="matmul_swiglu",
    )(x, w_gu, w_gu)


def _mm_res_ln_kernel(h_ref, w_ref, x_ref, g_ref, b_ref, o_ref, ob_ref):
    f = jnp.dot(h_ref[...], w_ref[...], preferred_element_type=F32)
    y = _layer_norm_rows(DEEPNORM_ALPHA * x_ref[...] + f, g_ref[...], b_ref[...])
    o_ref[...] = y
    ob_ref[...] = y.astype(BF16)


def _matmul_res_ln(h, w, x, g, b, tm):
    m, k = h.shape
    d = w.shape[1]
    return pl.pallas_call(
        _mm_res_ln_kernel,
        grid=(m // tm,),
        in_specs=[pl.BlockSpec((tm, k), lambda i: (i, 0)), pl.BlockSpec((k, d), lambda i: (0, 0)),
                  pl.BlockSpec((tm, d), lambda i: (i, 0)),
                  pl.BlockSpec((1, d), lambda i: (0, 0)), pl.BlockSpec((1, d), lambda i: (0, 0))],
        out_specs=[pl.BlockSpec((tm, d), lambda i: (i, 0)), pl.BlockSpec((tm, d), lambda i: (i, 0))],
        out_shape=[jax.ShapeDtypeStruct((m, d), F32), jax.ShapeDtypeStruct((m, d), BF16)],
        compiler_params=_cparams("arbitrary"),
        name="matmul_res_ln",
    )(h, w, x, g.reshape(1, d), b.reshape(1, d))


def _add2_res_ln_kernel(y0_ref, y1_ref, x_ref, g_ref, b_ref, o_ref, ob_ref):
    f = y0_ref[...] + y1_ref[...]
    y = _layer_norm_rows(DEEPNORM_ALPHA * x_ref[...] + f, g_ref[...], b_ref[...])
    o_ref[...] = y
    ob_ref[...] = y.astype(BF16)


def _add2_res_ln(y0, y1, x, g, b, tm):
    m, d = x.shape
    row = pl.BlockSpec((tm, d), lambda i: (i, 0))
    vec = pl.BlockSpec((1, d), lambda i: (0, 0))
    return pl.pallas_call(
        _add2_res_ln_kernel,
        grid=(m // tm,),
        in_specs=[row, row, row, vec, vec],
        out_specs=[row, row],
        out_shape=[jax.ShapeDtypeStruct((m, d), F32), jax.ShapeDtypeStruct((m, d), BF16)],
        compiler_params=_cparams("arbitrary"),
        name="moe_combine_res_ln",
    )(y0, y1, x, g.reshape(1, d), b.reshape(1, d))


def _gdn_gates(ba, alog_row, dtb_row, valid):
    beta = _sigmoid(ba)
    g = -jnp.exp(alog_row) * _softplus(ba + dtb_row)
    if valid is not None:
        beta = jnp.where(valid, beta, 0.0)
        g = jnp.where(valid, g, 0.0)
    return beta, g


def _group_cumsum(g, group):
    r_in = lax.broadcasted_iota(jnp.int32, g.shape, 0) % group
    s = 1
    while s < group:
        g = g + jnp.where(r_in >= s, pltpu.roll(g, s, axis=0), 0.0)
        s *= 2
    return g


def _group_last(g, group):
    rows = g.shape[0]
    r_in = lax.broadcasted_iota(jnp.int32, g.shape, 0) % group
    x = jnp.where(r_in == group - 1, g, 0.0)
    s = 1
    while s < group:
        x = x + jnp.where(r_in + s < group, pltpu.roll(x, rows - s, axis=0), 0.0)
        s *= 2
    return x


def _l2norm_rows(t):
    return t * lax.rsqrt(jnp.sum(t * t, axis=-1, keepdims=True) + 1e-6)


def _wy_block(q, k, v, beta_c, g_c, g_r, g_end_c, incl, strict, levels):
    c = q.shape[0]
    decay = jnp.where(incl, jnp.exp(jnp.where(incl, g_c - g_r, 0.0)), 0.0)
    kb = k * beta_c
    vb = v * beta_c
    a = jnp.where(strict, _bdot_nt(kb, k) * decay, 0.0)
    eye = (lax.broadcasted_iota(jnp.int32, (c, c), 0) == lax.broadcasted_iota(jnp.int32, (c, c), 1)).astype(F32)
    n = -a
    t = eye + n
    for _ in range(levels - 1):
        n = _bdot(n, n)
        t = t + _bdot(t, n)
    eg = jnp.exp(g_c)
    uw = _bdot(t, jnp.concatenate([vb, kb * eg], axis=1))
    u, w = uw[:, :A_DV], uw[:, A_DV:]
    qk = jnp.where(incl, _bdot_nt(q, k) * decay, 0.0)
    qg = q * eg
    kdec = k * jnp.exp(g_end_c - g_c)
    return u, w, qk, qg, kdec


def _gate_out(o, z, norm_w):
    o = o * lax.rsqrt(jnp.mean(o * o, axis=-1, keepdims=True) + RMS_EPS) * norm_w
    return o * _silu(z)


def _conv_head_inputs(xbuf_ref, cw_ref, row0, rows, h):
    outs = []
    for part in range(3):
        c0 = part * A_HEADS * A_DK + h * A_DK
        acc = xbuf_ref[pl.ds(row0 - 3, rows), pl.ds(c0, A_DK)] * cw_ref[0:1, pl.ds(c0, A_DK)]
        for j in range(1, CONV_W):
            acc = acc + xbuf_ref[pl.ds(row0 - 3 + j, rows), pl.ds(c0, A_DK)] * cw_ref[j:j + 1, pl.ds(c0, A_DK)]
        outs.append(_silu(acc))
    q = _l2norm_rows(outs[0]) * (A_DK ** -0.5)
    k = _l2norm_rows(outs[1])
    return q, k, outs[2]


def _gdn_prompt_kernel(proj_ref, cw_ref, alog_ref, dtb_ref, nw_ref, o_ref, s_out_ref, xbuf_ref, s_ref):
    i = pl.program_id(1)

    @pl.when(i == 0)
    def _():
        xbuf_ref[0:8, :] = jnp.zeros((8, CONV_DIM), F32)
        s_ref[...] = jnp.zeros(s_ref.shape, F32)

    pos = i * ROWS + lax.broadcasted_iota(jnp.int32, (ROWS, 1), 0)
    valid = pos >= LEAD
    xbuf_ref[8:8 + ROWS, :] = jnp.where(valid, proj_ref[:, 0:CONV_DIM], 0.0)

    beta, g = _gdn_gates(proj_ref[:, A_MAIN:A_PROJ_PAD], alog_ref[...], dtb_ref[...], valid)
    gcum = _group_cumsum(g, CHUNK)
    gcum_t = gcum.T

    ri = lax.broadcasted_iota(jnp.int32, (CHUNK, CHUNK), 0)
    ci = lax.broadcasted_iota(jnp.int32, (CHUNK, CHUNK), 1)
    incl = ri >= ci
    strict = ri > ci

    for h in range(A_HEADS):
        q, k, v = _conv_head_inputs(xbuf_ref, cw_ref, 8, ROWS, h)
        s = s_ref[h]
        o_parts = []
        for c in range(ROWS // CHUNK):
            r0, r1 = c * CHUNK, (c + 1) * CHUNK
            g_c = gcum[r0:r1, 8 + h:9 + h]
            g_r = gcum_t[8 + h:9 + h, r0:r1]
            g_end = gcum[r1 - 1:r1, 8 + h:9 + h]
            u, w, qk, qg, kdec = _wy_block(q[r0:r1], k[r0:r1], v[r0:r1], beta[r0:r1, h:h + 1], g_c, g_r,
                                           g_end, incl, strict, 6)
            ws = _bdot(jnp.concatenate([w, qg], axis=0), s)
            v_new = u - ws[:CHUNK]
            o_parts.append(ws[CHUNK:] + _bdot(qk, v_new))
            s = s * jnp.exp(g_end) + _bdot_tn(kdec, v_new)
        s_ref[h] = s
        o = jnp.concatenate(o_parts, axis=0)
        z = proj_ref[:, pl.ds(CONV_DIM + h * A_DV, A_DV)]
        o_ref[:, pl.ds(h * A_DV, A_DV)] = _gate_out(o, z, nw_ref[...]).astype(o_ref.dtype)

    xbuf_ref[0:8, :] = xbuf_ref[ROWS:ROWS + 8, :]

    @pl.when(i == pl.num_programs(1) - 1)
    def _():
        s_out_ref[0] = s_ref[...]


def _gdn_prompt(proj, conv_w, alog_row, dtb_row, norm_w, n_batch, lp):
    nblk = lp // ROWS
    vec = lambda n: pl.BlockSpec((1, n), lambda b, i: (0, 0))
    return pl.pallas_call(
        _gdn_prompt_kernel,
        grid=(n_batch, nblk),
        in_specs=[pl.BlockSpec((ROWS, A_PROJ_PAD), lambda b, i: (b * nblk + i, 0)),
                  pl.BlockSpec((CONV_W, CONV_DIM), lambda b, i: (0, 0)),
                  vec(128), vec(128), vec(A_DV)],
        out_specs=[pl.BlockSpec((ROWS, D_MODEL), lambda b, i: (b * nblk + i, 0)),
                   pl.BlockSpec((1, A_HEADS, A_DK, A_DV), lambda b, i: (b, 0, 0, 0))],
        out_shape=[jax.ShapeDtypeStruct((n_batch * lp, D_MODEL), BF16),
                   jax.ShapeDtypeStruct((n_batch, A_HEADS, A_DK, A_DV), F32)],
        scratch_shapes=[pltpu.VMEM((ROWS + 8, CONV_DIM), F32), pltpu.VMEM((A_HEADS, A_DK, A_DV), F32)],
        compiler_params=_cparams("arbitrary", "arbitrary"),
        name="gdn_prompt",
    )(proj, conv_w, alog_row, dtb_row, norm_w.reshape(1, A_DV))


def _gdn_sample_kernel(proj_ref, c0_ref, s0_ref, cw_ref, alog_ref, dtb_ref, nw_ref, o_ref, s_out_ref, xbuf_ref):
    nb, steps = SAMPLE_BB, CHUNK // SAMPLE_BB
    for b in range(nb):
        xbuf_ref[pl.ds(b * 16 + 5, 3), :] = c0_ref[b]
        xbuf_ref[pl.ds(b * 16 + 8, steps), :] = proj_ref[pl.ds(b * steps, steps), 0:CONV_DIM]

    beta, g = _gdn_gates(proj_ref[:, A_MAIN:A_PROJ_PAD], alog_ref[...], dtb_ref[...], None)
    gcum = _group_cumsum(g, steps)
    gcum_t = gcum.T
    gend = _group_last(gcum, steps)

    ri = lax.broadcasted_iota(jnp.int32, (CHUNK, CHUNK), 0)
    ci = lax.broadcasted_iota(jnp.int32, (CHUNK, CHUNK), 1)
    same = (ri // steps) == (ci // steps)
    incl = same & (ri >= ci)
    strict = same & (ri > ci)

    for h in range(A_HEADS):
        qs, ks, vs = [], [], []
        for b in range(nb):
            qb, kb_, vb_ = _conv_head_inputs(xbuf_ref, cw_ref, b * 16 + 8, steps, h)
            qs.append(qb), ks.append(kb_), vs.append(vb_)
        q, k, v = (jnp.concatenate(t, axis=0) for t in (qs, ks, vs))
        g_c = gcum[:, 8 + h:9 + h]
        g_r = gcum_t[8 + h:9 + h, :]
        g_e = gend[:, 8 + h:9 + h]
        u, w, qk, qg, kdec = _wy_block(q, k, v, beta[:, h:h + 1], g_c, g_r, g_e, incl, strict, 3)
        v_news, o_s = [], []
        for b in range(nb):
            r0, r1 = b * steps, (b + 1) * steps
            s = s0_ref[b, h]
            ws = _bdot(jnp.concatenate([w[r0:r1], qg[r0:r1]], axis=0), s)
            v_new = u[r0:r1] - ws[:steps]
            v_news.append(v_new)
            o_s.append(ws[steps:])
            s_out_ref[b, h] = s * jnp.exp(g_e[r1 - 1:r1, :]) + _bdot_tn(kdec[r0:r1], v_new)
        o = jnp.concatenate(o_s, axis=0) + _bdot(qk, jnp.concatenate(v_news, axis=0))
        z = proj_ref[:, pl.ds(CONV_DIM + h * A_DV, A_DV)]
        o_ref[:, pl.ds(h * A_DV, A_DV)] = _gate_out(o, z, nw_ref[...]).astype(o_ref.dtype)


def _gdn_sample(proj, conv0, s0, conv_w, alog_row, dtb_row, norm_w, row0, n_batch):
    blk0 = row0 // CHUNK
    vec = lambda n: pl.BlockSpec((1, n), lambda i: (0, 0))
    st = pl.BlockSpec((SAMPLE_BB, A_HEADS, A_DK, A_DV), lambda i: (i, 0, 0, 0))
    return pl.pallas_call(
        _gdn_sample_kernel,
        grid=(n_batch // SAMPLE_BB,),
        in_specs=[pl.BlockSpec((CHUNK, A_PROJ_PAD), lambda i: (blk0 + i, 0)),
                  pl.BlockSpec((SAMPLE_BB, CONV_W - 1, CONV_DIM), lambda i: (i, 0, 0)),
                  st,
                  pl.BlockSpec((CONV_W, CONV_DIM), lambda i: (0, 0)),
                  vec(128), vec(128), vec(A_DV)],
        out_specs=[pl.BlockSpec((CHUNK, D_MODEL), lambda i: (i, 0)), st],
        out_shape=[jax.ShapeDtypeStruct((n_batch * (CHUNK // SAMPLE_BB), D_MODEL), BF16),
                   jax.ShapeDtypeStruct(s0.shape, F32)],
        scratch_shapes=[pltpu.VMEM((SAMPLE_BB * 16, CONV_DIM), F32)],
        compiler_params=_cparams("arbitrary"),
        name="gdn_sample",
    )(proj, conv0, s0, conv_w, alog_row, dtb_row, norm_w.reshape(1, A_DV))


def _alibi_slope(h):
    return 2.0 ** (-8.0 * (h + 1) / B_HEADS)


def _sink_softmax_pv(pieces, sink):
    m = sink
    for s, _ in pieces:
        m = jnp.maximum(m, jnp.max(s, axis=-1, keepdims=True))
    den = jnp.exp(sink - m)
    acc = None
    ps = []
    for s, _ in pieces:
        p = jnp.exp(s - m)
        den = den + jnp.sum(p, axis=-1, keepdims=True)
        ps.append(p)
    inv = 1.0 / den
    for p, (_, v) in zip(ps, pieces):
        t = _bdot(p * inv, v)
        acc = t if acc is None else acc + t
    return acc


def _attn_prompt_kernel(sink_ref, q_ref, kvp_ref, kvc_ref, o_ref):
    i = pl.program_id(1)
    r = lax.broadcasted_iota(jnp.int32, (ROWS, 2 * ROWS), 0)
    c = lax.broadcasted_iota(jnp.int32, (ROWS, 2 * ROWS), 1)
    dist = r - c + ROWS
    kpos = i * ROWS - ROWS - LEAD + c
    valid = (kpos >= 0) & (dist >= 0) & (dist <= WINDOW)
    dist_f = dist.astype(F32)
    for g in range(B_KV_HEADS):
        k = jnp.concatenate([kvp_ref[:, pl.ds(g * B_HD, B_HD)], kvc_ref[:, pl.ds(g * B_HD, B_HD)]], axis=0)
        v = jnp.concatenate([kvp_ref[:, pl.ds(B_KV + g * B_HD, B_HD)], kvc_ref[:, pl.ds(B_KV + g * B_HD, B_HD)]], axis=0)
        kb, vb = k.astype(BF16), v.astype(BF16)
        for j in range(B_GROUP):
            h = g * B_GROUP + j
            s = _bdot_nt(q_ref[:, pl.ds(h * B_HD, B_HD)], kb) * (B_HD ** -0.5)
            s = jnp.where(valid, s - _alibi_slope(h) * dist_f, NEG_INF)
            o = _sink_softmax_pv([(s, vb)], sink_ref[h])
            o_ref[:, pl.ds(h * B_HD, B_HD)] = o.astype(o_ref.dtype)


def _attn_prompt(q, kv, sinks, n_batch, lp):
    nblk = lp // ROWS
    return pl.pallas_call(
        _attn_prompt_kernel,
        grid_spec=pltpu.PrefetchScalarGridSpec(
            num_scalar_prefetch=1,
            grid=(n_batch, nblk),
            in_specs=[pl.BlockSpec((ROWS, D_MODEL), lambda b, i, s: (b * nblk + i, 0)),
                      pl.BlockSpec((ROWS, 2 * B_KV), lambda b, i, s: (b * nblk + jnp.maximum(i - 1, 0), 0)),
                      pl.BlockSpec((ROWS, 2 * B_KV), lambda b, i, s: (b * nblk + i, 0))],
            out_specs=pl.BlockSpec((ROWS, D_MODEL), lambda b, i, s: (b * nblk + i, 0)),
        ),
        out_shape=jax.ShapeDtypeStruct((n_batch * lp, D_MODEL), BF16),
        compiler_params=_cparams("arbitrary", "arbitrary"),
        name="attn_prompt",
    )(sinks, q, kv, kv)


ATT_BB = 8
ATT_S = 8


def _attn_sample_kernel(sink_ref, q_ref, kvn_ref, ck_ref, cv_ref, o_ref):
    rows = B_GROUP * ATT_S
    t = lax.broadcasted_iota(jnp.int32, (rows, WINDOW), 0) % ATT_S
    c = lax.broadcasted_iota(jnp.int32, (rows, WINDOW), 1)
    dist_c = WINDOW + t - c
    valid_c = dist_c <= WINDOW
    tn = lax.broadcasted_iota(jnp.int32, (rows, ATT_S), 0) % ATT_S
    cn = lax.broadcasted_iota(jnp.int32, (rows, ATT_S), 1)
    dist_n = tn - cn
    valid_n = dist_n >= 0
    hrow = lax.broadcasted_iota(jnp.int32, (rows, 1), 0) // ATT_S
    for b in range(ATT_BB):
        r0 = b * ATT_S
        for g in range(B_KV_HEADS):
            slope = jnp.zeros((rows, 1), F32)
            sink = jnp.zeros((rows, 1), F32)
            for j in range(B_GROUP):
                slope = jnp.where(hrow == j, _alibi_slope(g * B_GROUP + j), slope)
                sink = jnp.where(hrow == j, sink_ref[g * B_GROUP + j], sink)
            q = jnp.concatenate([q_ref[pl.ds(r0, ATT_S), pl.ds((g * B_GROUP + j) * B_HD, B_HD)]
                                 for j in range(B_GROUP)], axis=0)
            kc = ck_ref[b, :, pl.ds(g * B_HD, B_HD)]
            vc = cv_ref[b, :, pl.ds(g * B_HD, B_HD)]
            kn = kvn_ref[pl.ds(r0, ATT_S), pl.ds(g * B_HD, B_HD)]
            vn = kvn_ref[pl.ds(r0, ATT_S), pl.ds(B_KV + g * B_HD, B_HD)]
            s_c = _bdot_nt(q, kc) * (B_HD ** -0.5)
            s_c = jnp.where(valid_c, s_c - slope * dist_c.astype(F32), NEG_INF)
            s_n = _bdot_nt(q, kn) * (B_HD ** -0.5)
            s_n = jnp.where(valid_n, s_n - slope * dist_n.astype(F32), NEG_INF)
            o = _sink_softmax_pv([(s_c, vc), (s_n, vn)], sink)
            for j in range(B_GROUP):
                h = g * B_GROUP + j
                o_ref[pl.ds(r0, ATT_S), pl.ds(h * B_HD, B_HD)] = o[j * ATT_S:(j + 1) * ATT_S].astype(o_ref.dtype)


def _attn_sample(q, kv, cache_k, cache_v, sinks, row0, n_batch):
    rows = ATT_BB * ATT_S
    blk0 = row0 // rows
    cache = pl.BlockSpec((ATT_BB, WINDOW, B_KV), lambda i, s: (i, 0, 0))
    return pl.pallas_call(
        _attn_sample_kernel,
        grid_spec=pltpu.PrefetchScalarGridSpec(
            num_scalar_prefetch=1,
            grid=(n_batch // ATT_BB,),
            in_specs=[pl.BlockSpec((rows, D_MODEL), lambda i, s: (blk0 + i, 0)),
                      pl.BlockSpec((rows, 2 * B_KV), lambda i, s: (blk0 + i, 0)),
                      cache, cache],
            out_specs=pl.BlockSpec((rows, D_MODEL), lambda i, s: (i, 0)),
        ),
        out_shape=jax.ShapeDtypeStruct((n_batch * ATT_S, D_MODEL), BF16),
        compiler_params=_cparams("arbitrary"),
        name="attn_sample",
    )(sinks, q, kv, cache_k, cache_v)


def _split3(x):
    hi = x.astype(BF16)
    r1 = x - hi.astype(F32)
    mid = r1.astype(BF16)
    lo = (r1 - mid.astype(F32)).astype(BF16)
    return hi, mid, lo


def _router_kernel(x_ref, w_ref, b_ref, idx_ref, gate_ref):
    xs = _split3(x_ref[...])
    ws = _split3(w_ref[...])
    logits = b_ref[...]
    for a, (i, j) in enumerate(((2, 2), (2, 1), (1, 2), (0, 2), (2, 0), (1, 1), (0, 1), (1, 0), (0, 0))):
        del a
        logits = logits + jnp.dot(xs[i], ws[j], preferred_element_type=F32)
    lane = lax.broadcasted_iota(jnp.int32, logits.shape, 1)
    real = lane < N_EXPERTS
    logits = jnp.where(real, logits, NEG_INF)
    m = jnp.max(logits, axis=-1, keepdims=True)
    e = jnp.where(real, jnp.exp(logits - m), 0.0)
    probs = e / jnp.sum(e, axis=-1, keepdims=True)
    p1 = jnp.max(probs, axis=-1, keepdims=True)
    i1 = jnp.min(jnp.where(probs == p1, lane, 128), axis=-1, keepdims=True)
    rest = jnp.where((lane == i1) | (~real), -1.0, probs)
    p2 = jnp.max(rest, axis=-1, keepdims=True)
    i2 = jnp.min(jnp.where(rest == p2, lane, 128), axis=-1, keepdims=True)
    tot = p1 + p2
    idx_ref[...] = jnp.where(lane == 0, i1, jnp.where(lane == 1, i2, 0))
    gate_ref[...] = jnp.where(lane == 0, p1 / tot, jnp.where(lane == 1, p2 / tot, 0.0))


def _router(x, w_router, b_router, tm):
    m, d = x.shape
    w = jnp.zeros((d, 128), F32).at[:, :N_EXPERTS].set(w_router)
    b = jnp.zeros((1, 128), F32).at[0, :N_EXPERTS].set(b_router)
    out = pl.BlockSpec((tm, 128), lambda i: (i, 0))
    return pl.pallas_call(
        _router_kernel,
        grid=(m // tm,),
        in_specs=[pl.BlockSpec((tm, d), lambda i: (i, 0)), pl.BlockSpec((d, 128), lambda i: (0, 0)),
                  pl.BlockSpec((1, 128), lambda i: (0, 0))],
        out_specs=[out, out],
        out_shape=[jax.ShapeDtypeStruct((m, 128), jnp.int32), jax.ShapeDtypeStruct((m, 128), F32)],
        compiler_params=_cparams("arbitrary"),
        name="moe_router",
    )(x, w, b)


def _moe_ffn_kernel(te_ref, xs_ref, wgu_ref, wd_ref, gate_ref, o_ref):
    del te_ref
    d_e = wd_ref.shape[1]
    gu = jnp.dot(xs_ref[...], wgu_ref[0], preferred_element_type=F32)
    h = (_silu(gu[:, :d_e]) * gu[:, d_e:]).astype(BF16)
    o_ref[...] = gate_ref[...] * jnp.dot(h, wd_ref[0], preferred_element_type=F32)


def _moe_ffn(tile_expert, xs, w_gu, w_down, row_gate):
    a_pad, d = xs.shape
    d_e = w_down.shape[1]
    return pl.pallas_call(
        _moe_ffn_kernel,
        grid_spec=pltpu.PrefetchScalarGridSpec(
            num_scalar_prefetch=1,
            grid=(a_pad // MOE_TM,),
            in_specs=[pl.BlockSpec((MOE_TM, d), lambda i, te: (i, 0)),
                      pl.BlockSpec((1, d, 2 * d_e), lambda i, te: (te[i], 0, 0)),
                      pl.BlockSpec((1, d_e, d), lambda i, te: (te[i], 0, 0)),
                      pl.BlockSpec((MOE_TM, 1), lambda i, te: (i, 0))],
            out_specs=pl.BlockSpec((MOE_TM, d), lambda i, te: (i, 0)),
        ),
        out_shape=jax.ShapeDtypeStruct((a_pad, d), F32),
        compiler_params=_cparams("arbitrary"),
        name="moe_experts",
    )(tile_expert, xs, w_gu, w_down, row_gate)


def _moe_layer(x, xb, w_router, b_router, w_gu, w_down, ln_g, ln_b, tm):
    tt = x.shape[0]
    idx, gate = _router(x, w_router, b_router, tm)
    e_flat = idx[:, :2].reshape(-1)
    g_flat = gate[:, :2].reshape(-1)
    onehot = (e_flat[:, None] == jnp.arange(N_EXPERTS, dtype=jnp.int32)[None, :]).astype(jnp.int32)
    csum = jnp.cumsum(onehot, axis=0)
    counts = csum[-1]
    rank = jnp.sum((csum - onehot) * onehot, axis=1)
    padded = ((counts + MOE_TM - 1) // MOE_TM) * MOE_TM
    pend = jnp.cumsum(padded)
    pstart = pend - padded
    dest = pstart[e_flat] + rank
    a_pad = 2 * tt + N_EXPERTS * MOE_TM
    src_tok = jnp.zeros((a_pad,), jnp.int32).at[dest].set(jnp.arange(2 * tt, dtype=jnp.int32) // 2)
    row_gate = jnp.zeros((a_pad,), F32).at[dest].set(g_flat)
    tile_start = jnp.arange(a_pad // MOE_TM, dtype=jnp.int32) * MOE_TM
    tile_expert = jnp.minimum(jnp.sum(tile_start[:, None] >= pend[None, :], axis=1), N_EXPERTS - 1).astype(jnp.int32)
    xs = jnp.take(xb, src_tok, axis=0)
    ys = _moe_ffn(tile_expert, xs, w_gu, w_down, row_gate.reshape(a_pad, 1))
    dest2 = dest.reshape(tt, 2)
    y0 = jnp.take(ys, dest2[:, 0], axis=0)
    y1 = jnp.take(ys, dest2[:, 1], axis=0)
    return _add2_res_ln(y0, y1, x, ln_g, ln_b, tm)


def kernel(x_prompt, x_sample, state_gdn, state_conv, cache_win_k, cache_win_v, meta_tokens, a_w_in, a_conv_w, a_A_log, a_dt_bias, a_norm_w, a_w_out, b_w_kv, b_w_q, b_sinks, b_w_o, ln_mix_g, ln_mix_b, ln_ffn_g, ln_ffn_b, ffd_w_gu, ffd_w_down, moe_w_router, moe_b_router, moe_w_gu, moe_w_down):
    nb, seq, d = x_prompt.shape
    nsb, steps, _ = x_sample.shape
    lp = LEAD + N_META + seq
    n_prompt = nb * lp
    tt = n_prompt + nsb * steps
    tm = 384 if tt % 384 == 0 else 128
    d_ff = ffd_w_down.shape[1]

    meta = jnp.broadcast_to(meta_tokens[None], (nb, N_META, d))
    xp = jnp.concatenate([jnp.zeros((nb, LEAD, d), F32), meta, x_prompt], axis=1)
    x = jnp.concatenate([xp.reshape(n_prompt, d), x_sample.reshape(nsb * steps, d)], axis=0)
    xb = x.astype(BF16)

    p_gdn, p_conv, s_gdn, s_conv = [], [], [], []
    kv = None
    for l in range(DEPTH):
        if l < N_A_LAYERS:
            w_in = jnp.concatenate([a_w_in[l], jnp.zeros((d, A_PROJ_PAD - a_w_in.shape[2]), F32)], axis=1).astype(BF16)
            proj = _matmul(xb, w_in, F32, tm, A_PROJ_PAD // 3)
            lane = jnp.arange(128)
            alog_row = jnp.zeros((1, 128), F32).at[0, 8:16].set(a_A_log[l])
            dtb_row = jnp.zeros((1, 128), F32).at[0, 8:16].set(a_dt_bias[l])
            del lane
            o_p, s_p = _gdn_prompt(proj, a_conv_w[l], alog_row, dtb_row, a_norm_w[l], nb, lp)
            o_s, s_s = _gdn_sample(proj, state_conv[l], state_gdn[l], a_conv_w[l], alog_row, dtb_row, a_norm_w[l],
                                   n_prompt, nsb)
            p_gdn.append(s_p)
            s_gdn.append(s_s)
            pq = proj[:n_prompt, :CONV_DIM].reshape(nb, lp, CONV_DIM)
            p_conv.append(pq[:, lp - (CONV_W - 1):])
            sq = proj[n_prompt:, :CONV_DIM].reshape(nsb, steps, CONV_DIM)
            s_conv.append(sq[:, steps - (CONV_W - 1):])
            mix_in = jnp.concatenate([o_p, o_s], axis=0)
            w_mix = a_w_out[l].astype(BF16)
        else:
            j = l - N_A_LAYERS
            if j == 0:
                kv = _matmul(xb, b_w_kv.astype(BF16), F32, tm, 2 * B_KV)
            q = _matmul(xb, b_w_q[j].astype(BF16), BF16, tm, d)
            o_p = _attn_prompt(q, kv, b_sinks[j], nb, lp)
            o_s = _attn_sample(q, kv, cache_win_k.reshape(nsb, WINDOW, B_KV), cache_win_v.reshape(nsb, WINDOW, B_KV),
                               b_sinks[j], n_prompt, nsb)
            mix_in = jnp.concatenate([o_p, o_s], axis=0)
            w_mix = b_w_o[j].astype(BF16)
        x, xb = _matmul_res_ln(mix_in, w_mix, x, ln_mix_g[l], ln_mix_b[l], tm)
        if l % 2 == 0:
            hmid = _matmul_swiglu(xb, ffd_w_gu[l // 2].astype(BF16), d_ff, tm, d_ff // 2)
            x, xb = _matmul_res_ln(hmid, ffd_w_down[l // 2].astype(BF16), x, ln_ffn_g[l], ln_ffn_b[l], tm)
        else:
            x, xb = _moe_layer(x, xb, moe_w_router[l // 2], moe_b_router[l // 2], moe_w_gu[l // 2].astype(BF16),
                               moe_w_down[l // 2].astype(BF16), ln_ffn_g[l], ln_ffn_b[l], tm)

    y_prompt = x[:n_prompt].reshape(nb, lp, d)[:, LEAD + N_META:]
    y_sample = x[n_prompt:].reshape(nsb, steps, d)
    kvp = kv[:n_prompt].reshape(nb, lp, 2, B_KV_HEADS, B_HD)[:, lp - WINDOW:]
    kvs = kv[n_prompt:].reshape(nsb, steps, 2, B_KV_HEADS, B_HD)
    s_wk = jnp.concatenate([cache_win_k[:, steps:], kvs[:, :, 0]], axis=1)
    s_wv = jnp.concatenate([cache_win_v[:, steps:], kvs[:, :, 1]], axis=1)
    return (y_prompt, y_sample, jnp.stack(p_gdn), jnp.stack(p_conv), kvp[:, :, 0], kvp[:, :, 1],
            jnp.stack(s_gdn), jnp.stack(s_conv), s_wk, s_wv)
```

```python
import functools
import math

import jax
import jax.numpy as jnp
from jax import lax
from jax.experimental import pallas as pl
from jax.experimental.pallas import tpu as pltpu

F32 = jnp.float32
BF16 = jnp.bfloat16

D_MODEL = 1024
N_META = 16
A_HEADS = 8
A_DK = 128
A_DV = 128
CONV_W = 4
CONV_DIM = A_HEADS * (2 * A_DK + A_DV)
A_MAIN = CONV_DIM + A_HEADS * A_DV
A_PROJ_PAD = A_MAIN + 128
B_HD = 64
B_HEADS = 16
B_KV_HEADS = 4
B_GROUP = 4
B_KV = B_KV_HEADS * B_HD
WINDOW = 128
N_EXPERTS = 8
DEPTH = 4
N_A_LAYERS = 2
DEEPNORM_ALPHA = (2 * DEPTH) ** 0.25
LN_EPS = 1e-5
RMS_EPS = 1e-6
NEG_INF = -1e30

ROWS = 128
CHUNK = 64
INV_BASE = 32
LEAD = (-N_META) % ROWS
SAMPLE_BB = CHUNK // 8
MOE_TM = 256
VMEM_LIMIT = 56 * 1024 * 1024


def _cparams(*sem):
    return pltpu.CompilerParams(dimension_semantics=sem, vmem_limit_bytes=VMEM_LIMIT)


def _bdot(a, b):
    return jnp.dot(a.astype(BF16), b.astype(BF16), preferred_element_type=F32)


def _bdot_nt(a, b):
    return lax.dot_general(a.astype(BF16), b.astype(BF16), (((1,), (1,)), ((), ())), preferred_element_type=F32)


def _bdot_tn(a, b):
    return lax.dot_general(a.astype(BF16), b.astype(BF16), (((0,), (0,)), ((), ())), preferred_element_type=F32)


def _sigmoid(x):
    return 1.0 / (1.0 + jnp.exp(-x))


def _silu(x):
    return x * _sigmoid(x)


def _softplus(x):
    return jnp.maximum(x, 0.0) + jnp.log(1.0 + jnp.exp(-jnp.abs(x)))


def _layer_norm_rows(v, g, b):
    mu = jnp.mean(v, axis=-1, keepdims=True)
    vc = v - mu
    var = jnp.mean(vc * vc, axis=-1, keepdims=True)
    return vc * lax.rsqrt(var + LN_EPS) * g + b


def _mm_kernel(x_ref, w_ref, o_ref):
    o_ref[...] = jnp.dot(x_ref[...], w_ref[...], preferred_element_type=F32).astype(o_ref.dtype)


def _matmul(x, w, out_dtype, tm, tn):
    m, k = x.shape
    n = w.shape[1]
    return pl.pallas_call(
        _mm_kernel,
        grid=(n // tn, m // tm),
        in_specs=[pl.BlockSpec((tm, k), lambda j, i: (i, 0)), pl.BlockSpec((k, tn), lambda j, i: (0, j))],
        out_specs=pl.BlockSpec((tm, tn), lambda j, i: (i, j)),
        out_shape=jax.ShapeDtypeStruct((m, n), out_dtype),
        compiler_params=_cparams("arbitrary", "arbitrary"),
        name="matmul",
    )(x, w)


def _swiglu_kernel(x_ref, wg_ref, wu_ref, o_ref):
    x = x_ref[...]
    g = jnp.dot(x, wg_ref[...], preferred_element_type=F32)
    u = jnp.dot(x, wu_ref[...], preferred_element_type=F32)
    o_ref[...] = (_silu(g) * u).astype(o_ref.dtype)


def _matmul_swiglu(x, w_gu, d_ff, tm, tn):
    m, k = x.shape
    nt = d_ff // tn
    return pl.pallas_call(
        _swiglu_kernel,
        grid=(nt, m // tm),
        in_specs=[pl.BlockSpec((tm, k), lambda j, i: (i, 0)),
                  pl.BlockSpec((k, tn), lambda j, i: (0, j)),
                  pl.BlockSpec((k, tn), lambda j, i: (0, j + nt))],
        out_specs=pl.BlockSpec((tm, tn), lambda j, i: (i, j)),
        out_shape=jax.ShapeDtypeStruct((m, d_ff), BF16),
        compiler_params=_cparams("arbitrary", "arbitrary"),
        name="matmul_swiglu",
    )(x, w_gu, w_gu)


def _mm_res_ln_kernel(h_ref, w_ref, x_ref, g_ref, b_ref, o_ref, ob_ref):
    f = jnp.dot(h_ref[...], w_ref[...], preferred_element_type=F32)
    y = _layer_norm_rows(DEEPNORM_ALPHA * x_ref[...] + f, g_ref[...], b_ref[...])
    o_ref[...] = y
    ob_ref[...] = y.astype(BF16)


def _matmul_res_ln(h, w, x, g, b, tm):
    m, k = h.shape
    d = w.shape[1]
    return pl.pallas_call(
        _mm_res_ln_kernel,
        grid=(m // tm,),
        in_specs=[pl.BlockSpec((tm, k), lambda i: (i, 0)), pl.BlockSpec((k, d), lambda i: (0, 0)),
                  pl.BlockSpec((tm, d), lambda i: (i, 0)),
                  pl.BlockSpec((1, d), lambda i: (0, 0)), pl.BlockSpec((1, d), lambda i: (0, 0))],
        out_specs=[pl.BlockSpec((tm, d), lambda i: (i, 0)), pl.BlockSpec((tm, d), lambda i: (i, 0))],
        out_shape=[jax.ShapeDtypeStruct((m, d), F32), jax.ShapeDtypeStruct((m, d), BF16)],
        compiler_params=_cparams("arbitrary"),
        name="matmul_res_ln",
    )(h, w, x, g.reshape(1, d), b.reshape(1, d))


def _add2_res_ln_kernel(y0_ref, y1_ref, gate_ref, x_ref, g_ref, b_ref, o_ref, ob_ref):
    f = gate_ref[:, 0:1] * y0_ref[...] + gate_ref[:, 1:2] * y1_ref[...]
    y = _layer_norm_rows(DEEPNORM_ALPHA * x_ref[...] + f, g_ref[...], b_ref[...])
    o_ref[...] = y
    ob_ref[...] = y.astype(BF16)


def _add2_res_ln(y0, y1, gate, x, g, b, tm):
    m, d = x.shape
    row = pl.BlockSpec((tm, d), lambda i: (i, 0))
    vec = pl.BlockSpec((1, d), lambda i: (0, 0))
    return pl.pallas_call(
        _add2_res_ln_kernel,
        grid=(m // tm,),
        in_specs=[row, row, pl.BlockSpec((tm, 128), lambda i: (i, 0)), row, vec, vec],
        out_specs=[row, row],
        out_shape=[jax.ShapeDtypeStruct((m, d), F32), jax.ShapeDtypeStruct((m, d), BF16)],
        compiler_params=_cparams("arbitrary"),
        name="moe_combine_res_ln",
    )(y0, y1, gate, x, g.reshape(1, d), b.reshape(1, d))


def _gdn_gates(ba, alog_row, dtb_row, valid):
    beta = _sigmoid(ba)
    g = -jnp.exp(alog_row) * _softplus(ba + dtb_row)
    if valid is not None:
        beta = jnp.where(valid, beta, 0.0)
        g = jnp.where(valid, g, 0.0)
    return beta, g


def _group_cumsum(g, group):
    r_in = lax.broadcasted_iota(jnp.int32, g.shape, 0) % group
    s = 1
    while s < group:
        g = g + jnp.where(r_in >= s, pltpu.roll(g, s, axis=0), 0.0)
        s *= 2
    return g


def _group_last(g, group):
    rows = g.shape[0]
    r_in = lax.broadcasted_iota(jnp.int32, g.shape, 0) % group
    x = jnp.where(r_in == group - 1, g, 0.0)
    s = 1
    while s < group:
        x = x + jnp.where(r_in + s < group, pltpu.roll(x, rows - s, axis=0), 0.0)
        s *= 2
    return x


def _l2norm_rows(t):
    return t * lax.rsqrt(jnp.sum(t * t, axis=-1, keepdims=True) + 1e-6)


def _wy_block(q, k, v, beta_c, g_c, g_r, g_end_c, incl, strict, levels):
    c = q.shape[0]
    decay = jnp.where(incl, jnp.exp(jnp.where(incl, g_c - g_r, 0.0)), 0.0)
    kb = k * beta_c
    vb = v * beta_c
    a = jnp.where(strict, _bdot_nt(kb, k) * decay, 0.0)
    eye = (lax.broadcasted_iota(jnp.int32, (c, c), 0) == lax.broadcasted_iota(jnp.int32, (c, c), 1)).astype(F32)
    n = -a
    t = eye + n
    for _ in range(levels - 1):
        n = _bdot(n, n)
        t = t + _bdot(t, n)
    eg = jnp.exp(g_c)
    uw = _bdot(t, jnp.concatenate([vb, kb * eg], axis=1))
    u, w = uw[:, :A_DV], uw[:, A_DV:]
    qk = jnp.where(incl, _bdot_nt(q, k) * decay, 0.0)
    qg = q * eg
    kdec = k * jnp.exp(g_end_c - g_c)
    return u, w, qk, qg, kdec


def _gate_out(o, z, norm_w):
    o = o * lax.rsqrt(jnp.mean(o * o, axis=-1, keepdims=True) + RMS_EPS) * norm_w
    return o * _silu(z)


def _conv_head_inputs(xbuf_ref, cw_ref, row0, rows, h):
    outs = []
    for part in range(3):
        c0 = part * A_HEADS * A_DK + h * A_DK
        acc = xbuf_ref[pl.ds(row0 - 3, rows), pl.ds(c0, A_DK)] * cw_ref[0:1, pl.ds(c0, A_DK)]
        for j in range(1, CONV_W):
            acc = acc + xbuf_ref[pl.ds(row0 - 3 + j, rows), pl.ds(c0, A_DK)] * cw_ref[j:j + 1, pl.ds(c0, A_DK)]
        outs.append(_silu(acc))
    q = _l2norm_rows(outs[0]) * (A_DK ** -0.5)
    k = _l2norm_rows(outs[1])
    return q, k, outs[2]


def _blockdiag(a, b):
    z = jnp.zeros_like(a)
    return jnp.concatenate([jnp.concatenate([a, z], axis=1), jnp.concatenate([z, b], axis=1)], axis=0)


def _gdn_prompt_kernel(proj_ref, cw_ref, alog_ref, dtb_ref, nw_ref, obuf_ref, o_ref, s_out_ref, c_out_ref,
                       xbuf_ref, s_ref):
    del obuf_ref
    i = pl.program_id(1)

    @pl.when(i == 0)
    def _():
        xbuf_ref[0:8, :] = jnp.zeros((8, CONV_DIM), F32)
        s_ref[...] = jnp.zeros(s_ref.shape, F32)

    pos = i * ROWS + lax.broadcasted_iota(jnp.int32, (ROWS, 1), 0)
    valid = pos >= LEAD
    xbuf_ref[8:8 + ROWS, :] = jnp.where(valid, proj_ref[:, 0:CONV_DIM], 0.0)

    beta, g = _gdn_gates(proj_ref[:, A_MAIN:A_PROJ_PAD], alog_ref[...], dtb_ref[...], valid)
    gcum = _group_cumsum(g, ROWS)
    gcum_t = gcum.T

    ri = lax.broadcasted_iota(jnp.int32, (ROWS, ROWS), 0)
    ci = lax.broadcasted_iota(jnp.int32, (ROWS, ROWS), 1)
    incl = ri >= ci
    strict = ri > ci

    a_s, qk_s, rhs_s, qg_s, kdec_s, gend_s = [], [], [], [], [], []
    for h in range(A_HEADS):
        q, k, v = _conv_head_inputs(xbuf_ref, cw_ref, 8, ROWS, h)
        g_c = gcum[:, 8 + h:9 + h]
        g_r = gcum_t[8 + h:9 + h, :]
        g_end = gcum[ROWS - 1:ROWS, 8 + h:9 + h]
        decay = jnp.where(incl, jnp.exp(jnp.where(incl, g_c - g_r, 0.0)), 0.0)
        bcol = beta[:, h:h + 1]
        kb = k * bcol
        eg = jnp.exp(g_c)
        kq = _bdot_nt(jnp.concatenate([kb, q], axis=0), k)
        a_s.append(jnp.where(strict, kq[:ROWS] * decay, 0.0))
        qk_s.append(jnp.where(incl, kq[ROWS:] * decay, 0.0).astype(BF16))
        rhs_s.append(jnp.concatenate([v * bcol, kb * eg], axis=1).astype(BF16))
        qg_s.append((q * eg).astype(BF16))
        kdec_s.append((k * jnp.exp(g_end - g_c)).astype(BF16))
        gend_s.append(g_end)

    r2 = lax.broadcasted_iota(jnp.int32, (2 * ROWS, 2 * ROWS), 0)
    c2 = lax.broadcasted_iota(jnp.int32, (2 * ROWS, 2 * ROWS), 1)
    eye = (r2 == c2).astype(F32)
    pairs = A_HEADS // 2
    a_bd = [_blockdiag(a_s[2 * p], a_s[2 * p + 1]) for p in range(pairs)]
    base = (r2 // INV_BASE) == (c2 // INV_BASE)
    nbf = [jnp.where(base, -a, 0.0).astype(BF16) for a in a_bd]
    t32 = [eye + n.astype(F32) for n in nbf]
    size = 2
    while size < INV_BASE:
        nbf = [jnp.dot(n, n, preferred_element_type=F32).astype(BF16) for n in nbf]
        t32 = [t + jnp.dot(t.astype(BF16), n, preferred_element_type=F32) for t, n in zip(t32, nbf)]
        size *= 2
    while size < ROWS:
        off = ((r2 // (2 * size)) == (c2 // (2 * size))) & ((r2 // size) != (c2 // size))
        tbf = [t.astype(BF16) for t in t32]
        xs = [jnp.dot(jnp.where(off, a, 0.0).astype(BF16), t, preferred_element_type=F32).astype(BF16)
              for a, t in zip(a_bd, tbf)]
        t32 = [t - jnp.dot(tb, x, preferred_element_type=F32) for t, tb, x in zip(t32, tbf, xs)]
        size *= 2
    uw_s = [jnp.dot(t32[p].astype(BF16), jnp.concatenate([rhs_s[2 * p], rhs_s[2 * p + 1]], axis=0),
                    preferred_element_type=F32) for p in range(pairs)]

    for h in range(A_HEADS):
        uw = uw_s[h // 2][(h % 2) * ROWS:(h % 2 + 1) * ROWS]
        u, w = uw[:, :A_DV], uw[:, A_DV:]
        s = s_ref[h]
        ws = jnp.dot(jnp.concatenate([w.astype(BF16), qg_s[h]], axis=0), s.astype(BF16), preferred_element_type=F32)
        v_new = (u - ws[:ROWS]).astype(BF16)
        o = ws[ROWS:] + jnp.dot(qk_s[h], v_new, preferred_element_type=F32)
        s_ref[h] = s * jnp.exp(gend_s[h]) + lax.dot_general(kdec_s[h], v_new, (((0,), (0,)), ((), ())),
                                                            preferred_element_type=F32)
        z = proj_ref[:, pl.ds(CONV_DIM + h * A_DV, A_DV)]
        o_ref[:, pl.ds(h * A_DV, A_DV)] = _gate_out(o, z, nw_ref[...]).astype(o_ref.dtype)

    xbuf_ref[0:8, :] = xbuf_ref[ROWS:ROWS + 8, :]

    @pl.when(i == pl.num_programs(1) - 1)
    def _():
        s_out_ref[0] = s_ref[...]
        c_out_ref[0] = xbuf_ref[ROWS:ROWS + 8, :]


def _gdn_prompt(proj, obuf, conv_w, alog_row, dtb_row, norm_w, n_batch, lp):
    nblk = lp // ROWS
    vec = lambda n: pl.BlockSpec((1, n), lambda b, i: (0, 0))
    return pl.pallas_call(
        _gdn_prompt_kernel,
        grid=(n_batch, nblk),
        in_specs=[pl.BlockSpec((ROWS, A_PROJ_PAD), lambda b, i: (b * nblk + i, 0)),
                  pl.BlockSpec((CONV_W, CONV_DIM), lambda b, i: (0, 0)),
                  vec(128), vec(128), vec(A_DV),
                  pl.BlockSpec(memory_space=pl.ANY)],
        out_specs=[pl.BlockSpec((ROWS, D_MODEL), lambda b, i: (b * nblk + i, 0)),
                   pl.BlockSpec((1, A_HEADS, A_DK, A_DV), lambda b, i: (b, 0, 0, 0)),
                   pl.BlockSpec((1, 8, CONV_DIM), lambda b, i: (b, 0, 0))],
        out_shape=[jax.ShapeDtypeStruct(obuf.shape, obuf.dtype),
                   jax.ShapeDtypeStruct((n_batch, A_HEADS, A_DK, A_DV), F32),
                   jax.ShapeDtypeStruct((n_batch, 8, CONV_DIM), F32)],
        scratch_shapes=[pltpu.VMEM((ROWS + 8, CONV_DIM), F32), pltpu.VMEM((A_HEADS, A_DK, A_DV), F32)],
        input_output_aliases={5: 0},
        compiler_params=_cparams("arbitrary", "arbitrary"),
        name="gdn_prompt",
    )(proj, conv_w, alog_row, dtb_row, norm_w.reshape(1, A_DV), obuf)


def _gdn_sample_kernel(proj_ref, c0_ref, s0_ref, cw_ref, alog_ref, dtb_ref, nw_ref, obuf_ref, o_ref, s_out_ref,
                       c_out_ref, xbuf_ref):
    del obuf_ref
    nb, steps = SAMPLE_BB, CHUNK // SAMPLE_BB
    for b in range(nb):
        xbuf_ref[pl.ds(b * 16 + 5, 3), :] = c0_ref[b]
        xbuf_ref[pl.ds(b * 16 + 8, steps), :] = proj_ref[pl.ds(b * steps, steps), 0:CONV_DIM]
        c_out_ref[b] = proj_ref[pl.ds((b + 1) * steps - (CONV_W - 1), CONV_W - 1), 0:CONV_DIM]

    beta, g = _gdn_gates(proj_ref[:, A_MAIN:A_PROJ_PAD], alog_ref[...], dtb_ref[...], None)
    gcum = _group_cumsum(g, steps)
    gcum_t = gcum.T
    gend = _group_last(gcum, steps)

    ri = lax.broadcasted_iota(jnp.int32, (CHUNK, CHUNK), 0)
    ci = lax.broadcasted_iota(jnp.int32, (CHUNK, CHUNK), 1)
    same = (ri // steps) == (ci // steps)
    incl = same & (ri >= ci)
    strict = same & (ri > ci)

    for h in range(A_HEADS):
        qs, ks, vs = [], [], []
        for b in range(nb):
            qb, kb_, vb_ = _conv_head_inputs(xbuf_ref, cw_ref, b * 16 + 8, steps, h)
            qs.append(qb), ks.append(kb_), vs.append(vb_)
        q, k, v = (jnp.concatenate(t, axis=0) for t in (qs, ks, vs))
        g_c = gcum[:, 8 + h:9 + h]
        g_r = gcum_t[8 + h:9 + h, :]
        g_e = gend[:, 8 + h:9 + h]
        u, w, qk, qg, kdec = _wy_block(q, k, v, beta[:, h:h + 1], g_c, g_r, g_e, incl, strict, 3)
        v_news, o_s = [], []
        for b in range(nb):
            r0, r1 = b * steps, (b + 1) * steps
            s = s0_ref[b, h]
            ws = _bdot(jnp.concatenate([w[r0:r1], qg[r0:r1]], axis=0), s)
            v_new = u[r0:r1] - ws[:steps]
            v_news.append(v_new)
            o_s.append(ws[steps:])
            s_out_ref[b, h] = s * jnp.exp(g_e[r1 - 1:r1, :]) + _bdot_tn(kdec[r0:r1], v_new)
        o = jnp.concatenate(o_s, axis=0) + _bdot(qk, jnp.concatenate(v_news, axis=0))
        z = proj_ref[:, pl.ds(CONV_DIM + h * A_DV, A_DV)]
        o_ref[:, pl.ds(h * A_DV, A_DV)] = _gate_out(o, z, nw_ref[...]).astype(o_ref.dtype)


def _gdn_sample(proj, obuf, conv0, s0, conv_w, alog_row, dtb_row, norm_w, row0, n_batch):
    blk0 = row0 // CHUNK
    vec = lambda n: pl.BlockSpec((1, n), lambda i: (0, 0))
    st = pl.BlockSpec((SAMPLE_BB, A_HEADS, A_DK, A_DV), lambda i: (i, 0, 0, 0))
    cs = pl.BlockSpec((SAMPLE_BB, CONV_W - 1, CONV_DIM), lambda i: (i, 0, 0))
    return pl.pallas_call(
        _gdn_sample_kernel,
        grid=(n_batch // SAMPLE_BB,),
        in_specs=[pl.BlockSpec((CHUNK, A_PROJ_PAD), lambda i: (blk0 + i, 0)),
                  cs, st,
                  pl.BlockSpec((CONV_W, CONV_DIM), lambda i: (0, 0)),
                  vec(128), vec(128), vec(A_DV),
                  pl.BlockSpec(memory_space=pl.ANY)],
        out_specs=[pl.BlockSpec((CHUNK, D_MODEL), lambda i: (blk0 + i, 0)), st, cs],
        out_shape=[jax.ShapeDtypeStruct(obuf.shape, obuf.dtype),
                   jax.ShapeDtypeStruct(s0.shape, F32),
                   jax.ShapeDtypeStruct(conv0.shape, F32)],
        scratch_shapes=[pltpu.VMEM((SAMPLE_BB * 16, CONV_DIM), F32)],
        input_output_aliases={7: 0},
        compiler_params=_cparams("arbitrary"),
        name="gdn_sample",
    )(proj, conv0, s0, conv_w, alog_row, dtb_row, norm_w.reshape(1, A_DV), obuf)


def _alibi_slope(h):
    return 2.0 ** (-8.0 * (h + 1) / B_HEADS)


def _sink_softmax_pv(pieces, sink):
    m = sink
    for s, _ in pieces:
        m = jnp.maximum(m, jnp.max(s, axis=-1, keepdims=True))
    den = jnp.exp(sink - m)
    acc = None
    ps = []
    for s, _ in pieces:
        p = jnp.exp(s - m)
        den = den + jnp.sum(p, axis=-1, keepdims=True)
        ps.append(p)
    inv = 1.0 / den
    for p, (_, v) in zip(ps, pieces):
        t = _bdot(p * inv, v)
        acc = t if acc is None else acc + t
    return acc


def _attn_prompt_kernel(sink_ref, q_ref, kvp_ref, kvc_ref, obuf_ref, o_ref):
    del obuf_ref
    i = pl.program_id(1)
    r = lax.broadcasted_iota(jnp.int32, (ROWS, 2 * ROWS), 0)
    c = lax.broadcasted_iota(jnp.int32, (ROWS, 2 * ROWS), 1)
    dist = r - c + ROWS
    kpos = i * ROWS - ROWS - LEAD + c
    valid = (kpos >= 0) & (dist >= 0) & (dist <= WINDOW)
    dist_f = dist.astype(F32)
    for g in range(B_KV_HEADS):
        k = jnp.concatenate([kvp_ref[:, pl.ds(g * B_HD, B_HD)], kvc_ref[:, pl.ds(g * B_HD, B_HD)]], axis=0)
        v = jnp.concatenate([kvp_ref[:, pl.ds(B_KV + g * B_HD, B_HD)], kvc_ref[:, pl.ds(B_KV + g * B_HD, B_HD)]], axis=0)
        kb, vb = k.astype(BF16), v.astype(BF16)
        for j in range(B_GROUP):
            h = g * B_GROUP + j
            s = _bdot_nt(q_ref[:, pl.ds(h * B_HD, B_HD)], kb) * (B_HD ** -0.5)
            s = jnp.where(valid, s - _alibi_slope(h) * dist_f, NEG_INF)
            o = _sink_softmax_pv([(s, vb)], sink_ref[h])
            o_ref[:, pl.ds(h * B_HD, B_HD)] = o.astype(o_ref.dtype)


def _attn_prompt(q, kv, obuf, sinks, n_batch, lp):
    nblk = lp // ROWS
    return pl.pallas_call(
        _attn_prompt_kernel,
        grid_spec=pltpu.PrefetchScalarGridSpec(
            num_scalar_prefetch=1,
            grid=(n_batch, nblk),
            in_specs=[pl.BlockSpec((ROWS, D_MODEL), lambda b, i, s: (b * nblk + i, 0)),
                      pl.BlockSpec((ROWS, 2 * B_KV), lambda b, i, s: (b * nblk + jnp.maximum(i - 1, 0), 0)),
                      pl.BlockSpec((ROWS, 2 * B_KV), lambda b, i, s: (b * nblk + i, 0)),
                      pl.BlockSpec(memory_space=pl.ANY)],
            out_specs=pl.BlockSpec((ROWS, D_MODEL), lambda b, i, s: (b * nblk + i, 0)),
        ),
        out_shape=jax.ShapeDtypeStruct(obuf.shape, obuf.dtype),
        input_output_aliases={4: 0},
        compiler_params=_cparams("arbitrary", "arbitrary"),
        name="attn_prompt",
    )(sinks, q, kv, kv, obuf)


ATT_BB = 8
ATT_S = 8


def _attn_sample_kernel(sink_ref, q_ref, kvn_ref, ck_ref, cv_ref, obuf_ref, o_ref):
    del obuf_ref
    rows = B_GROUP * ATT_S
    t = lax.broadcasted_iota(jnp.int32, (rows, WINDOW), 0) % ATT_S
    c = lax.broadcasted_iota(jnp.int32, (rows, WINDOW), 1)
    dist_c = WINDOW + t - c
    valid_c = dist_c <= WINDOW
    tn = lax.broadcasted_iota(jnp.int32, (rows, ATT_S), 0) % ATT_S
    cn = lax.broadcasted_iota(jnp.int32, (rows, ATT_S), 1)
    dist_n = tn - cn
    valid_n = dist_n >= 0
    hrow = lax.broadcasted_iota(jnp.int32, (rows, 1), 0) // ATT_S
    for b in range(ATT_BB):
        r0 = b * ATT_S
        for g in range(B_KV_HEADS):
            slope = jnp.zeros((rows, 1), F32)
            sink = jnp.zeros((rows, 1), F32)
            for j in range(B_GROUP):
                slope = jnp.where(hrow == j, _alibi_slope(g * B_GROUP + j), slope)
                sink = jnp.where(hrow == j, sink_ref[g * B_GROUP + j], sink)
            q = jnp.concatenate([q_ref[pl.ds(r0, ATT_S), pl.ds((g * B_GROUP + j) * B_HD, B_HD)]
                                 for j in range(B_GROUP)], axis=0)
            kc = ck_ref[b, :, pl.ds(g * B_HD, B_HD)]
            vc = cv_ref[b, :, pl.ds(g * B_HD, B_HD)]
            kn = kvn_ref[pl.ds(r0, ATT_S), pl.ds(g * B_HD, B_HD)]
            vn = kvn_ref[pl.ds(r0, ATT_S), pl.ds(B_KV + g * B_HD, B_HD)]
            s_c = _bdot_nt(q, kc) * (B_HD ** -0.5)
            s_c = jnp.where(valid_c, s_c - slope * dist_c.astype(F32), NEG_INF)
            s_n = _bdot_nt(q, kn) * (B_HD ** -0.5)
            s_n = jnp.where(valid_n, s_n - slope * dist_n.astype(F32), NEG_INF)
            o = _sink_softmax_pv([(s_c, vc), (s_n, vn)], sink)
            for j in range(B_GROUP):
                h = g * B_GROUP + j
                o_ref[pl.ds(r0, ATT_S), pl.ds(h * B_HD, B_HD)] = o[j * ATT_S:(j + 1) * ATT_S].astype(o_ref.dtype)


def _attn_sample(q, kv, obuf, cache_k, cache_v, sinks, row0, n_batch):
    rows = ATT_BB * ATT_S
    blk0 = row0 // rows
    cache = pl.BlockSpec((ATT_BB, WINDOW, B_KV), lambda i, s: (i, 0, 0))
    return pl.pallas_call(
        _attn_sample_kernel,
        grid_spec=pltpu.PrefetchScalarGridSpec(
            num_scalar_prefetch=1,
            grid=(n_batch // ATT_BB,),
            in_specs=[pl.BlockSpec((rows, D_MODEL), lambda i, s: (blk0 + i, 0)),
                      pl.BlockSpec((rows, 2 * B_KV), lambda i, s: (blk0 + i, 0)),
                      cache, cache,
                      pl.BlockSpec(memory_space=pl.ANY)],
            out_specs=pl.BlockSpec((rows, D_MODEL), lambda i, s: (blk0 + i, 0)),
        ),
        out_shape=jax.ShapeDtypeStruct(obuf.shape, obuf.dtype),
        input_output_aliases={5: 0},
        compiler_params=_cparams("arbitrary"),
        name="attn_sample",
    )(sinks, q, kv, cache_k, cache_v, obuf)


def _split3(x):
    hi = x.astype(BF16)
    r1 = x - hi.astype(F32)
    mid = r1.astype(BF16)
    lo = (r1 - mid.astype(F32)).astype(BF16)
    return hi, mid, lo


def _router_kernel(x_ref, w_ref, b_ref, idx_ref, gate_ref):
    xs = _split3(x_ref[...])
    ws = _split3(w_ref[...])
    logits = b_ref[...]
    for a, (i, j) in enumerate(((2, 2), (2, 1), (1, 2), (0, 2), (2, 0), (1, 1), (0, 1), (1, 0), (0, 0))):
        del a
        logits = logits + jnp.dot(xs[i], ws[j], preferred_element_type=F32)
    lane = lax.broadcasted_iota(jnp.int32, logits.shape, 1)
    real = lane < N_EXPERTS
    logits = jnp.where(real, logits, NEG_INF)
    m = jnp.max(logits, axis=-1, keepdims=True)
    e = jnp.where(real, jnp.exp(logits - m), 0.0)
    probs = e / jnp.sum(e, axis=-1, keepdims=True)
    p1 = jnp.max(probs, axis=-1, keepdims=True)
    i1 = jnp.min(jnp.where(probs == p1, lane, 128), axis=-1, keepdims=True)
    rest = jnp.where((lane == i1) | (~real), -1.0, probs)
    p2 = jnp.max(rest, axis=-1, keepdims=True)
    i2 = jnp.min(jnp.where(rest == p2, lane, 128), axis=-1, keepdims=True)
    tot = p1 + p2
    idx_ref[...] = jnp.where(lane == 0, i1, jnp.where(lane == 1, i2, 0))
    gate_ref[...] = jnp.where(lane == 0, p1 / tot, jnp.where(lane == 1, p2 / tot, 0.0))


def _router(x, w_router, b_router, tm):
    m, d = x.shape
    w = jnp.zeros((d, 128), F32).at[:, :N_EXPERTS].set(w_router)
    b = jnp.zeros((1, 128), F32).at[0, :N_EXPERTS].set(b_router)
    out = pl.BlockSpec((tm, 128), lambda i: (i, 0))
    return pl.pallas_call(
        _router_kernel,
        grid=(m // tm,),
        in_specs=[pl.BlockSpec((tm, d), lambda i: (i, 0)), pl.BlockSpec((d, 128), lambda i: (0, 0)),
                  pl.BlockSpec((1, 128), lambda i: (0, 0))],
        out_specs=[out, out],
        out_shape=[jax.ShapeDtypeStruct((m, 128), jnp.int32), jax.ShapeDtypeStruct((m, 128), F32)],
        compiler_params=_cparams("arbitrary"),
        name="moe_router",
    )(x, w, b)


def _moe_ffn_kernel(te_ref, xs_ref, wgu_ref, wd_ref, o_ref):
    del te_ref
    d_e = wd_ref.shape[1]
    gu = jnp.dot(xs_ref[...], wgu_ref[0], preferred_element_type=F32)
    h = (_silu(gu[:, :d_e]) * gu[:, d_e:]).astype(BF16)
    o_ref[...] = jnp.dot(h, wd_ref[0], preferred_element_type=F32)


def _moe_ffn(tile_expert, xs, w_gu, w_down):
    a_pad, d = xs.shape
    d_e = w_down.shape[1]
    return pl.pallas_call(
        _moe_ffn_kernel,
        grid_spec=pltpu.PrefetchScalarGridSpec(
            num_scalar_prefetch=1,
            grid=(a_pad // MOE_TM,),
            in_specs=[pl.BlockSpec((MOE_TM, d), lambda i, te: (i, 0)),
                      pl.BlockSpec((1, d, 2 * d_e), lambda i, te: (te[i], 0, 0)),
                      pl.BlockSpec((1, d_e, d), lambda i, te: (te[i], 0, 0))],
            out_specs=pl.BlockSpec((MOE_TM, d), lambda i, te: (i, 0)),
        ),
        out_shape=jax.ShapeDtypeStruct((a_pad, d), F32),
        compiler_params=_cparams("arbitrary"),
        name="moe_experts",
    )(tile_expert, xs, w_gu, w_down)


def _moe_layer(x, xb, w_router, b_router, w_gu, w_down, ln_g, ln_b, tm):
    tt = x.shape[0]
    idx, gate = _router(x, w_router, b_router, tm)
    e_flat = idx[:, :2].reshape(-1)
    onehot = (e_flat[:, None] == jnp.arange(N_EXPERTS, dtype=jnp.int32)[None, :]).astype(jnp.int32)
    csum = jnp.cumsum(onehot, axis=0)
    counts = csum[-1]
    rank = jnp.sum((csum - onehot) * onehot, axis=1)
    padded = ((counts + MOE_TM - 1) // MOE_TM) * MOE_TM
    pend = jnp.cumsum(padded)
    pstart = pend - padded
    dest = pstart[e_flat] + rank
    a_pad = 2 * tt + N_EXPERTS * MOE_TM
    src_tok = jnp.zeros((a_pad,), jnp.int32).at[dest].set(jnp.arange(2 * tt, dtype=jnp.int32) // 2)
    tile_start = jnp.arange(a_pad // MOE_TM, dtype=jnp.int32) * MOE_TM
    tile_expert = jnp.minimum(jnp.sum(tile_start[:, None] >= pend[None, :], axis=1), N_EXPERTS - 1).astype(jnp.int32)
    xs = jnp.take(xb, src_tok, axis=0)
    ys = _moe_ffn(tile_expert, xs, w_gu, w_down)
    dest2 = dest.reshape(tt, 2)
    y0 = jnp.take(ys, dest2[:, 0], axis=0)
    y1 = jnp.take(ys, dest2[:, 1], axis=0)
    return _add2_res_ln(y0, y1, gate, x, ln_g, ln_b, tm)


def kernel(x_prompt, x_sample, state_gdn, state_conv, cache_win_k, cache_win_v, meta_tokens, a_w_in, a_conv_w, a_A_log, a_dt_bias, a_norm_w, a_w_out, b_w_kv, b_w_q, b_sinks, b_w_o, ln_mix_g, ln_mix_b, ln_ffn_g, ln_ffn_b, ffd_w_gu, ffd_w_down, moe_w_router, moe_b_router, moe_w_gu, moe_w_down):
    nb, seq, d = x_prompt.shape
    nsb, steps, _ = x_sample.shape
    lp = LEAD + N_META + seq
    n_prompt = nb * lp
    tt = n_prompt + nsb * steps
    tm = 384 if tt % 384 == 0 else 128
    d_ff = ffd_w_down.shape[1]

    meta = jnp.broadcast_to(meta_tokens[None], (nb, N_META, d))
    xp = jnp.concatenate([jnp.zeros((nb, LEAD, d), F32), meta, x_prompt], axis=1)
    x = jnp.concatenate([xp.reshape(n_prompt, d), x_sample.reshape(nsb * steps, d)], axis=0)
    xb = x.astype(BF16)

    p_gdn, p_conv, s_gdn, s_conv = [], [], [], []
    kv = None
    for l in range(DEPTH):
        if l < N_A_LAYERS:
            w_in = jnp.concatenate([a_w_in[l], jnp.zeros((d, A_PROJ_PAD - a_w_in.shape[2]), F32)], axis=1).astype(BF16)
            proj = _matmul(xb, w_in, F32, tm, A_PROJ_PAD // 3)
            alog_row = jnp.zeros((1, 128), F32).at[0, 8:16].set(a_A_log[l])
            dtb_row = jnp.zeros((1, 128), F32).at[0, 8:16].set(a_dt_bias[l])
            mix_in = jnp.zeros((tt, d), BF16)
            mix_in, s_p, c_p = _gdn_prompt(proj, mix_in, a_conv_w[l], alog_row, dtb_row, a_norm_w[l], nb, lp)
            mix_in, s_s, c_s = _gdn_sample(proj, mix_in, state_conv[l], state_gdn[l], a_conv_w[l], alog_row, dtb_row,
                                           a_norm_w[l], n_prompt, nsb)
            p_gdn.append(s_p)
            s_gdn.append(s_s)
            p_conv.append(c_p[:, 8 - (CONV_W - 1):])
            s_conv.append(c_s)
            w_mix = a_w_out[l].astype(BF16)
        else:
            j = l - N_A_LAYERS
            if j == 0:
                kv = _matmul(xb, b_w_kv.astype(BF16), F32, tm, 2 * B_KV)
            q = _matmul(xb, b_w_q[j].astype(BF16), BF16, tm, d)
            mix_in = jnp.zeros((tt, d), BF16)
            mix_in = _attn_prompt(q, kv, mix_in, b_sinks[j], nb, lp)
            mix_in = _attn_sample(q, kv, mix_in, cache_win_k.reshape(nsb, WINDOW, B_KV),
                                  cache_win_v.reshape(nsb, WINDOW, B_KV), b_sinks[j], n_prompt, nsb)
            w_mix = b_w_o[j].astype(BF16)
        x, xb = _matmul_res_ln(mix_in, w_mix, x, ln_mix_g[l], ln_mix_b[l], tm)
        if l % 2 == 0:
            hmid = _matmul_swiglu(xb, ffd_w_gu[l // 2].astype(BF16), d_ff, tm, d_ff // 2)
            x, xb = _matmul_res_ln(hmid, ffd_w_down[l // 2].astype(BF16), x, ln_ffn_g[l], ln_ffn_b[l], tm)
        else:
            x, xb = _moe_layer(x, xb, moe_w_router[l // 2], moe_b_router[l // 2], moe_w_gu[l // 2].astype(BF16),
                               moe_w_down[l // 2].astype(BF16), ln_ffn_g[l], ln_ffn_b[l], tm)

    y_prompt = x[:n_prompt].reshape(nb, lp, d)[:, LEAD + N_META:]
    y_sample = x[n_prompt:].reshape(nsb, steps, d)
    kvp = jnp.stack([kv[(b + 1) * lp - WINDOW:(b + 1) * lp] for b in range(nb)]).reshape(nb, WINDOW, 2, B_KV_HEADS, B_HD)
    kvs = kv[n_prompt:].reshape(nsb, steps, 2, B_KV_HEADS, B_HD)
    s_wk = jnp.concatenate([cache_win_k[:, steps:], kvs[:, :, 0]], axis=1)
    s_wv = jnp.concatenate([cache_win_v[:, steps:], kvs[:, :, 1]], axis=1)
    return (y_prompt, y_sample, jnp.stack(p_gdn), jnp.stack(p_conv), kvp[:, :, 0], kvp[:, :, 1],
            jnp.stack(s_gdn), jnp.stack(s_conv), s_wk, s_wv)
```

```python
import functools
import math

import jax
import jax.numpy as jnp
from jax import lax
from jax.experimental import pallas as pl
from jax.experimental.pallas import tpu as pltpu

F32 = jnp.float32
BF16 = jnp.bfloat16

D_MODEL = 1024
N_META = 16
A_HEADS = 8
A_DK = 128
A_DV = 128
CONV_W = 4
CONV_DIM = A_HEADS * (2 * A_DK + A_DV)
A_MAIN = CONV_DIM + A_HEADS * A_DV
A_PROJ_PAD = A_MAIN + 128
B_HD = 64
B_HEADS = 16
B_KV_HEADS = 4
B_GROUP = 4
B_KV = B_KV_HEADS * B_HD
WINDOW = 128
N_EXPERTS = 8
DEPTH = 4
N_A_LAYERS = 2
DEEPNORM_ALPHA = (2 * DEPTH) ** 0.25
LN_EPS = 1e-5
RMS_EPS = 1e-6
NEG_INF = -1e30

ROWS = 128
CHUNK = 64
INV_BASE = 32
LEAD = (-N_META) % ROWS
SAMPLE_BB = CHUNK // 8
MOE_TM = 256
VMEM_LIMIT = 56 * 1024 * 1024


def _cparams(*sem):
    return pltpu.CompilerParams(dimension_semantics=sem, vmem_limit_bytes=VMEM_LIMIT)


def _bdot(a, b):
    return jnp.dot(a.astype(BF16), b.astype(BF16), preferred_element_type=F32)


def _bdot_nt(a, b):
    return lax.dot_general(a.astype(BF16), b.astype(BF16), (((1,), (1,)), ((), ())), preferred_element_type=F32)


def _bdot_tn(a, b):
    return lax.dot_general(a.astype(BF16), b.astype(BF16), (((0,), (0,)), ((), ())), preferred_element_type=F32)


def _sigmoid(x):
    return 1.0 / (1.0 + jnp.exp(-x))


def _silu(x):
    return x * _sigmoid(x)


def _softplus(x):
    return jnp.maximum(x, 0.0) + jnp.log(1.0 + jnp.exp(-jnp.abs(x)))


def _layer_norm_rows(v, g, b):
    mu = jnp.mean(v, axis=-1, keepdims=True)
    vc = v - mu
    var = jnp.mean(vc * vc, axis=-1, keepdims=True)
    return vc * lax.rsqrt(var + LN_EPS) * g + b


def _mm_kernel(x_ref, w_ref, o_ref):
    o_ref[...] = jnp.dot(x_ref[...], w_ref[...], preferred_element_type=F32).astype(o_ref.dtype)


def _matmul(x, w, out_dtype, tm, tn):
    m, k = x.shape
    n = w.shape[1]
    return pl.pallas_call(
        _mm_kernel,
        grid=(n // tn, m // tm),
        in_specs=[pl.BlockSpec((tm, k), lambda j, i: (i, 0)), pl.BlockSpec((k, tn), lambda j, i: (0, j))],
        out_specs=pl.BlockSpec((tm, tn), lambda j, i: (i, j)),
        out_shape=jax.ShapeDtypeStruct((m, n), out_dtype),
        compiler_params=_cparams("arbitrary", "arbitrary"),
        name="matmul",
    )(x, w)


def _swiglu_kernel(x_ref, wg_ref, wu_ref, o_ref):
    x = x_ref[...]
    g = jnp.dot(x, wg_ref[...], preferred_element_type=F32)
    u = jnp.dot(x, wu_ref[...], preferred_element_type=F32)
    o_ref[...] = (_silu(g) * u).astype(o_ref.dtype)


def _matmul_swiglu(x, w_gu, d_ff, tm, tn):
    m, k = x.shape
    nt = d_ff // tn
    return pl.pallas_call(
        _swiglu_kernel,
        grid=(nt, m // tm),
        in_specs=[pl.BlockSpec((tm, k), lambda j, i: (i, 0)),
                  pl.BlockSpec((k, tn), lambda j, i: (0, j)),
                  pl.BlockSpec((k, tn), lambda j, i: (0, j + nt))],
        out_specs=pl.BlockSpec((tm, tn), lambda j, i: (i, j)),
        out_shape=jax.ShapeDtypeStruct((m, d_ff), BF16),
        compiler_params=_cparams("arbitrary", "arbitrary"),
        name="matmul_swiglu",
    )(x, w_gu, w_gu)


def _mm_res_ln_kernel(h_ref, w_ref, x_ref, g_ref, b_ref, o_ref, ob_ref):
    f = jnp.dot(h_ref[...], w_ref[...], preferred_element_type=F32)
    y = _layer_norm_rows(DEEPNORM_ALPHA * x_ref[...] + f, g_ref[...], b_ref[...])
    o_ref[...] = y
    ob_ref[...] = y.astype(BF16)


def _matmul_res_ln(h, w, x, g, b, tm):
    m, k = h.shape
    d = w.shape[1]
    return pl.pallas_call(
        _mm_res_ln_kernel,
        grid=(m // tm,),
        in_specs=[pl.BlockSpec((tm, k), lambda i: (i, 0)), pl.BlockSpec((k, d), lambda i: (0, 0)),
                  pl.BlockSpec((tm, d), lambda i: (i, 0)),
                  pl.BlockSpec((1, d), lambda i: (0, 0)), pl.BlockSpec((1, d), lambda i: (0, 0))],
        out_specs=[pl.BlockSpec((tm, d), lambda i: (i, 0)), pl.BlockSpec((tm, d), lambda i: (i, 0))],
        out_shape=[jax.ShapeDtypeStruct((m, d), F32), jax.ShapeDtypeStruct((m, d), BF16)],
        compiler_params=_cparams("arbitrary"),
        name="matmul_res_ln",
    )(h, w, x, g.reshape(1, d), b.reshape(1, d))


def _add2_res_ln_kernel(y0_ref, y1_ref, gate_ref, x_ref, g_ref, b_ref, o_ref, ob_ref):
    f = gate_ref[:, 0:1] * y0_ref[...] + gate_ref[:, 1:2] * y1_ref[...]
    y = _layer_norm_rows(DEEPNORM_ALPHA * x_ref[...] + f, g_ref[...], b_ref[...])
    o_ref[...] = y
    ob_ref[...] = y.astype(BF16)


def _add2_res_ln(y0, y1, gate, x, g, b, tm):
    m, d = x.shape
    row = pl.BlockSpec((tm, d), lambda i: (i, 0))
    vec = pl.BlockSpec((1, d), lambda i: (0, 0))
    return pl.pallas_call(
        _add2_res_ln_kernel,
        grid=(m // tm,),
        in_specs=[row, row, pl.BlockSpec((tm, 128), lambda i: (i, 0)), row, vec, vec],
        out_specs=[row, row],
        out_shape=[jax.ShapeDtypeStruct((m, d), F32), jax.ShapeDtypeStruct((m, d), BF16)],
        compiler_params=_cparams("arbitrary"),
        name="moe_combine_res_ln",
    )(y0, y1, gate, x, g.reshape(1, d), b.reshape(1, d))


def _gdn_gates(ba, alog_row, dtb_row, valid):
    beta = _sigmoid(ba)
    g = -jnp.exp(alog_row) * _softplus(ba + dtb_row)
    if valid is not None:
        beta = jnp.where(valid, beta, 0.0)
        g = jnp.where(valid, g, 0.0)
    return beta, g


def _group_cumsum(g, group):
    r_in = lax.broadcasted_iota(jnp.int32, g.shape, 0) % group
    s = 1
    while s < group:
        g = g + jnp.where(r_in >= s, pltpu.roll(g, s, axis=0), 0.0)
        s *= 2
    return g


def _group_last(g, group):
    rows = g.shape[0]
    r_in = lax.broadcasted_iota(jnp.int32, g.shape, 0) % group
    x = jnp.where(r_in == group - 1, g, 0.0)
    s = 1
    while s < group:
        x = x + jnp.where(r_in + s < group, pltpu.roll(x, rows - s, axis=0), 0.0)
        s *= 2
    return x


def _l2norm_rows(t):
    return t * lax.rsqrt(jnp.sum(t * t, axis=-1, keepdims=True) + 1e-6)


def _wy_block(q, k, v, beta_c, g_c, g_r, g_end_c, incl, strict, levels):
    c = q.shape[0]
    decay = jnp.where(incl, jnp.exp(jnp.where(incl, g_c - g_r, 0.0)), 0.0)
    kb = k * beta_c
    vb = v * beta_c
    a = jnp.where(strict, _bdot_nt(kb, k) * decay, 0.0)
    eye = (lax.broadcasted_iota(jnp.int32, (c, c), 0) == lax.broadcasted_iota(jnp.int32, (c, c), 1)).astype(F32)
    n = -a
    t = eye + n
    for _ in range(levels - 1):
        n = _bdot(n, n)
        t = t + _bdot(t, n)
    eg = jnp.exp(g_c)
    uw = _bdot(t, jnp.concatenate([vb, kb * eg], axis=1))
    u, w = uw[:, :A_DV], uw[:, A_DV:]
    qk = jnp.where(incl, _bdot_nt(q, k) * decay, 0.0)
    qg = q * eg
    kdec = k * jnp.exp(g_end_c - g_c)
    return u, w, qk, qg, kdec


def _gate_out(o, z, norm_w):
    o = o * lax.rsqrt(jnp.mean(o * o, axis=-1, keepdims=True) + RMS_EPS) * norm_w
    return o * _silu(z)


def _conv_head_inputs(xbuf_ref, cw_ref, row0, rows, h):
    outs = []
    for part in range(3):
        c0 = part * A_HEADS * A_DK + h * A_DK
        acc = xbuf_ref[pl.ds(row0 - 3, rows), pl.ds(c0, A_DK)] * cw_ref[0:1, pl.ds(c0, A_DK)]
        for j in range(1, CONV_W):
            acc = acc + xbuf_ref[pl.ds(row0 - 3 + j, rows), pl.ds(c0, A_DK)] * cw_ref[j:j + 1, pl.ds(c0, A_DK)]
        outs.append(_silu(acc))
    q = _l2norm_rows(outs[0]) * (A_DK ** -0.5)
    k = _l2norm_rows(outs[1])
    return q, k, outs[2]


def _blockdiag(a, b):
    z = jnp.zeros_like(a)
    return jnp.concatenate([jnp.concatenate([a, z], axis=1), jnp.concatenate([z, b], axis=1)], axis=0)


def _gdn_prompt_kernel(proj_ref, cw_ref, alog_ref, dtb_ref, nw_ref, obuf_ref, o_ref, s_out_ref, c_out_ref,
                       xbuf_ref, s_ref):
    del obuf_ref
    i = pl.program_id(1)

    @pl.when(i == 0)
    def _():
        xbuf_ref[0:8, :] = jnp.zeros((8, CONV_DIM), F32)
        s_ref[...] = jnp.zeros(s_ref.shape, F32)

    pos = i * ROWS + lax.broadcasted_iota(jnp.int32, (ROWS, 1), 0)
    valid = pos >= LEAD
    xbuf_ref[8:8 + ROWS, :] = jnp.where(valid, proj_ref[:, 0:CONV_DIM], 0.0)

    beta, g = _gdn_gates(proj_ref[:, A_MAIN:A_PROJ_PAD], alog_ref[...], dtb_ref[...], valid)
    gcum = _group_cumsum(g, ROWS)
    gcum_t = gcum.T

    ri = lax.broadcasted_iota(jnp.int32, (ROWS, ROWS), 0)
    ci = lax.broadcasted_iota(jnp.int32, (ROWS, ROWS), 1)
    incl = ri >= ci
    strict = ri > ci

    a_s, qk_s, rhs_s, qg_s, kdec_s, gend_s = [], [], [], [], [], []
    for h in range(A_HEADS):
        q, k, v = _conv_head_inputs(xbuf_ref, cw_ref, 8, ROWS, h)
        g_c = gcum[:, 8 + h:9 + h]
        g_r = gcum_t[8 + h:9 + h, :]
        g_end = gcum[ROWS - 1:ROWS, 8 + h:9 + h]
        decay = jnp.where(incl, jnp.exp(jnp.where(incl, g_c - g_r, 0.0)), 0.0)
        bcol = beta[:, h:h + 1]
        kb = k * bcol
        eg = jnp.exp(g_c)
        kq = _bdot_nt(jnp.concatenate([kb, q], axis=0), k)
        a_s.append(jnp.where(strict, kq[:ROWS] * decay, 0.0))
        qk_s.append(jnp.where(incl, kq[ROWS:] * decay, 0.0).astype(BF16))
        rhs_s.append(jnp.concatenate([v * bcol, kb * eg], axis=1).astype(BF16))
        qg_s.append((q * eg).astype(BF16))
        kdec_s.append((k * jnp.exp(g_end - g_c)).astype(BF16))
        gend_s.append(g_end)

    r2 = lax.broadcasted_iota(jnp.int32, (2 * ROWS, 2 * ROWS), 0)
    c2 = lax.broadcasted_iota(jnp.int32, (2 * ROWS, 2 * ROWS), 1)
    eye = (r2 == c2).astype(F32)
    pairs = A_HEADS // 2
    a_bd = [_blockdiag(a_s[2 * p], a_s[2 * p + 1]) for p in range(pairs)]
    base = (r2 // INV_BASE) == (c2 // INV_BASE)
    nbf = [jnp.where(base, -a, 0.0).astype(BF16) for a in a_bd]
    t32 = [eye + n.astype(F32) for n in nbf]
    size = 2
    while size < INV_BASE:
        nbf = [jnp.dot(n, n, preferred_element_type=F32).astype(BF16) for n in nbf]
        t32 = [t + jnp.dot(t.astype(BF16), n, preferred_element_type=F32) for t, n in zip(t32, nbf)]
        size *= 2
    while size < ROWS:
        off = ((r2 // (2 * size)) == (c2 // (2 * size))) & ((r2 // size) != (c2 // size))
        tbf = [t.astype(BF16) for t in t32]
        xs = [jnp.dot(jnp.where(off, a, 0.0).astype(BF16), t, preferred_element_type=F32).astype(BF16)
              for a, t in zip(a_bd, tbf)]
        t32 = [t - jnp.dot(tb, x, preferred_element_type=F32) for t, tb, x in zip(t32, tbf, xs)]
        size *= 2
    uw_s = [jnp.dot(t32[p].astype(BF16), jnp.concatenate([rhs_s[2 * p], rhs_s[2 * p + 1]], axis=0),
                    preferred_element_type=F32) for p in range(pairs)]

    for h in range(A_HEADS):
        uw = uw_s[h // 2][(h % 2) * ROWS:(h % 2 + 1) * ROWS]
        u, w = uw[:, :A_DV], uw[:, A_DV:]
        s = s_ref[h]
        ws = jnp.dot(jnp.concatenate([w.astype(BF16), qg_s[h]], axis=0), s.astype(BF16), preferred_element_type=F32)
        v_new = (u - ws[:ROWS]).astype(BF16)
        o = ws[ROWS:] + jnp.dot(qk_s[h], v_new, preferred_element_type=F32)
        s_ref[h] = s * jnp.exp(gend_s[h]) + lax.dot_general(kdec_s[h], v_new, (((0,), (0,)), ((), ())),
                                                            preferred_element_type=F32)
        z = proj_ref[:, pl.ds(CONV_DIM + h * A_DV, A_DV)]
        o_ref[:, pl.ds(h * A_DV, A_DV)] = _gate_out(o, z, nw_ref[...]).astype(o_ref.dtype)

    xbuf_ref[0:8, :] = xbuf_ref[ROWS:ROWS + 8, :]

    @pl.when(i == pl.num_programs(1) - 1)
    def _():
        s_out_ref[0] = s_ref[...]
        c_out_ref[0] = xbuf_ref[ROWS:ROWS + 8, :]


def _gdn_prompt(proj, obuf, conv_w, alog_row, dtb_row, norm_w, n_batch, lp):
    nblk = lp // ROWS
    vec = lambda n: pl.BlockSpec((1, n), lambda b, i: (0, 0))
    return pl.pallas_call(
        _gdn_prompt_kernel,
        grid=(n_batch, nblk),
        in_specs=[pl.BlockSpec((ROWS, A_PROJ_PAD), lambda b, i: (b * nblk + i, 0)),
                  pl.BlockSpec((CONV_W, CONV_DIM), lambda b, i: (0, 0)),
                  vec(128), vec(128), vec(A_DV),
                  pl.BlockSpec(memory_space=pl.ANY)],
        out_specs=[pl.BlockSpec((ROWS, D_MODEL), lambda b, i: (b * nblk + i, 0)),
                   pl.BlockSpec((1, A_HEADS, A_DK, A_DV), lambda b, i: (b, 0, 0, 0)),
                   pl.BlockSpec((1, 8, CONV_DIM), lambda b, i: (b, 0, 0))],
        out_shape=[jax.ShapeDtypeStruct(obuf.shape, obuf.dtype),
                   jax.ShapeDtypeStruct((n_batch, A_HEADS, A_DK, A_DV), F32),
                   jax.ShapeDtypeStruct((n_batch, 8, CONV_DIM), F32)],
        scratch_shapes=[pltpu.VMEM((ROWS + 8, CONV_DIM), F32), pltpu.VMEM((A_HEADS, A_DK, A_DV), F32)],
        input_output_aliases={5: 0},
        compiler_params=_cparams("arbitrary", "arbitrary"),
        name="gdn_prompt",
    )(proj, conv_w, alog_row, dtb_row, norm_w.reshape(1, A_DV), obuf)


def _gdn_sample_kernel(proj_ref, c0_ref, s0_ref, cw_ref, alog_ref, dtb_ref, nw_ref, obuf_ref, o_ref, s_out_ref,
                       c_out_ref, xbuf_ref):
    del obuf_ref
    nb, steps = SAMPLE_BB, CHUNK // SAMPLE_BB
    for b in range(nb):
        xbuf_ref[pl.ds(b * 16 + 5, 3), :] = c0_ref[b]
        xbuf_ref[pl.ds(b * 16 + 8, steps), :] = proj_ref[pl.ds(b * steps, steps), 0:CONV_DIM]
        c_out_ref[b] = proj_ref[pl.ds((b + 1) * steps - (CONV_W - 1), CONV_W - 1), 0:CONV_DIM]

    beta, g = _gdn_gates(proj_ref[:, A_MAIN:A_PROJ_PAD], alog_ref[...], dtb_ref[...], None)
    gcum = _group_cumsum(g, steps)
    gcum_t = gcum.T
    gend = _group_last(gcum, steps)

    ri = lax.broadcasted_iota(jnp.int32, (CHUNK, CHUNK), 0)
    ci = lax.broadcasted_iota(jnp.int32, (CHUNK, CHUNK), 1)
    same = (ri // steps) == (ci // steps)
    incl = same & (ri >= ci)
    strict = same & (ri > ci)

    for h in range(A_HEADS):
        qs, ks, vs = [], [], []
        for b in range(nb):
            qb, kb_, vb_ = _conv_head_inputs(xbuf_ref, cw_ref, b * 16 + 8, steps, h)
            qs.append(qb), ks.append(kb_), vs.append(vb_)
        q, k, v = (jnp.concatenate(t, axis=0) for t in (qs, ks, vs))
        g_c = gcum[:, 8 + h:9 + h]
        g_r = gcum_t[8 + h:9 + h, :]
        g_e = gend[:, 8 + h:9 + h]
        u, w, qk, qg, kdec = _wy_block(q, k, v, beta[:, h:h + 1], g_c, g_r, g_e, incl, strict, 3)
        v_news, o_s = [], []
        for b in range(nb):
            r0, r1 = b * steps, (b + 1) * steps
            s = s0_ref[b, h]
            ws = _bdot(jnp.concatenate([w[r0:r1], qg[r0:r1]], axis=0), s)
            v_new = u[r0:r1] - ws[:steps]
            v_news.append(v_new)
            o_s.append(ws[steps:])
            s_out_ref[b, h] = s * jnp.exp(g_e[r1 - 1:r1, :]) + _bdot_tn(kdec[r0:r1], v_new)
        o = jnp.concatenate(o_s, axis=0) + _bdot(qk, jnp.concatenate(v_news, axis=0))
        z = proj_ref[:, pl.ds(CONV_DIM + h * A_DV, A_DV)]
        o_ref[:, pl.ds(h * A_DV, A_DV)] = _gate_out(o, z, nw_ref[...]).astype(o_ref.dtype)


def _gdn_sample(proj, obuf, conv0, s0, conv_w, alog_row, dtb_row, norm_w, row0, n_batch):
    blk0 = row0 // CHUNK
    vec = lambda n: pl.BlockSpec((1, n), lambda i: (0, 0))
    st = pl.BlockSpec((SAMPLE_BB, A_HEADS, A_DK, A_DV), lambda i: (i, 0, 0, 0))
    cs = pl.BlockSpec((SAMPLE_BB, CONV_W - 1, CONV_DIM), lambda i: (i, 0, 0))
    return pl.pallas_call(
        _gdn_sample_kernel,
        grid=(n_batch // SAMPLE_BB,),
        in_specs=[pl.BlockSpec((CHUNK, A_PROJ_PAD), lambda i: (blk0 + i, 0)),
                  cs, st,
                  pl.BlockSpec((CONV_W, CONV_DIM), lambda i: (0, 0)),
                  vec(128), vec(128), vec(A_DV),
                  pl.BlockSpec(memory_space=pl.ANY)],
        out_specs=[pl.BlockSpec((CHUNK, D_MODEL), lambda i: (blk0 + i, 0)), st, cs],
        out_shape=[jax.ShapeDtypeStruct(obuf.shape, obuf.dtype),
                   jax.ShapeDtypeStruct(s0.shape, F32),
                   jax.ShapeDtypeStruct(conv0.shape, F32)],
        scratch_shapes=[pltpu.VMEM((SAMPLE_BB * 16, CONV_DIM), F32)],
        input_output_aliases={7: 0},
        compiler_params=_cparams("arbitrary"),
        name="gdn_sample",
    )(proj, conv0, s0, conv_w, alog_row, dtb_row, norm_w.reshape(1, A_DV), obuf)


def _alibi_slope(h):
    return 2.0 ** (-8.0 * (h + 1) / B_HEADS)


def _sink_softmax_pv(pieces, sink):
    m = sink
    for s, _ in pieces:
        m = jnp.maximum(m, jnp.max(s, axis=-1, keepdims=True))
    den = jnp.exp(sink - m)
    acc = None
    ps = []
    for s, _ in pieces:
        p = jnp.exp(s - m)
        den = den + jnp.sum(p, axis=-1, keepdims=True)
        ps.append(p)
    inv = 1.0 / den
    for p, (_, v) in zip(ps, pieces):
        t = _bdot(p * inv, v)
        acc = t if acc is None else acc + t
    return acc


def _attn_prompt_kernel(sink_ref, q_ref, kvp_ref, kvc_ref, obuf_ref, o_ref):
    del obuf_ref
    i = pl.program_id(1)
    r = lax.broadcasted_iota(jnp.int32, (ROWS, 2 * ROWS), 0)
    c = lax.broadcasted_iota(jnp.int32, (ROWS, 2 * ROWS), 1)
    dist = r - c + ROWS
    kpos = i * ROWS - ROWS - LEAD + c
    valid = (kpos >= 0) & (dist >= 0) & (dist <= WINDOW)
    dist_f = dist.astype(F32)
    for g in range(B_KV_HEADS):
        k = jnp.concatenate([kvp_ref[:, pl.ds(g * B_HD, B_HD)], kvc_ref[:, pl.ds(g * B_HD, B_HD)]], axis=0)
        v = jnp.concatenate([kvp_ref[:, pl.ds(B_KV + g * B_HD, B_HD)], kvc_ref[:, pl.ds(B_KV + g * B_HD, B_HD)]], axis=0)
        kb, vb = k.astype(BF16), v.astype(BF16)
        for j in range(B_GROUP):
            h = g * B_GROUP + j
            s = _bdot_nt(q_ref[:, pl.ds(h * B_HD, B_HD)], kb) * (B_HD ** -0.5)
            s = jnp.where(valid, s - _alibi_slope(h) * dist_f, NEG_INF)
            o = _sink_softmax_pv([(s, vb)], sink_ref[h])
            o_ref[:, pl.ds(h * B_HD, B_HD)] = o.astype(o_ref.dtype)


def _attn_prompt(q, kv, obuf, sinks, n_batch, lp):
    nblk = lp // ROWS
    return pl.pallas_call(
        _attn_prompt_kernel,
        grid_spec=pltpu.PrefetchScalarGridSpec(
            num_scalar_prefetch=1,
            grid=(n_batch, nblk),
            in_specs=[pl.BlockSpec((ROWS, D_MODEL), lambda b, i, s: (b * nblk + i, 0)),
                      pl.BlockSpec((ROWS, 2 * B_KV), lambda b, i, s: (b * nblk + jnp.maximum(i - 1, 0), 0)),
                      pl.BlockSpec((ROWS, 2 * B_KV), lambda b, i, s: (b * nblk + i, 0)),
                      pl.BlockSpec(memory_space=pl.ANY)],
            out_specs=pl.BlockSpec((ROWS, D_MODEL), lambda b, i, s: (b * nblk + i, 0)),
        ),
        out_shape=jax.ShapeDtypeStruct(obuf.shape, obuf.dtype),
        input_output_aliases={4: 0},
        compiler_params=_cparams("arbitrary", "arbitrary"),
        name="attn_prompt",
    )(sinks, q, kv, kv, obuf)


ATT_BB = 8
ATT_S = 8


def _attn_sample_kernel(sink_ref, q_ref, kvn_ref, ck_ref, cv_ref, obuf_ref, o_ref):
    del obuf_ref
    rows = B_GROUP * ATT_S
    t = lax.broadcasted_iota(jnp.int32, (rows, WINDOW), 0) % ATT_S
    c = lax.broadcasted_iota(jnp.int32, (rows, WINDOW), 1)
    dist_c = WINDOW + t - c
    valid_c = dist_c <= WINDOW
    tn = lax.broadcasted_iota(jnp.int32, (rows, ATT_S), 0) % ATT_S
    cn = lax.broadcasted_iota(jnp.int32, (rows, ATT_S), 1)
    dist_n = tn - cn
    valid_n = dist_n >= 0
    hrow = lax.broadcasted_iota(jnp.int32, (rows, 1), 0) // ATT_S
    for b in range(ATT_BB):
        r0 = b * ATT_S
        for g in range(B_KV_HEADS):
            slope = jnp.zeros((rows, 1), F32)
            sink = jnp.zeros((rows, 1), F32)
            for j in range(B_GROUP):
                slope = jnp.where(hrow == j, _alibi_slope(g * B_GROUP + j), slope)
                sink = jnp.where(hrow == j, sink_ref[g * B_GROUP + j], sink)
            q = jnp.concatenate([q_ref[pl.ds(r0, ATT_S), pl.ds((g * B_GROUP + j) * B_HD, B_HD)]
                                 for j in range(B_GROUP)], axis=0)
            kc = ck_ref[b, :, pl.ds(g * B_HD, B_HD)]
            vc = cv_ref[b, :, pl.ds(g * B_HD, B_HD)]
            kn = kvn_ref[pl.ds(r0, ATT_S), pl.ds(g * B_HD, B_HD)]
            vn = kvn_ref[pl.ds(r0, ATT_S), pl.ds(B_KV + g * B_HD, B_HD)]
            s_c = _bdot_nt(q, kc) * (B_HD ** -0.5)
            s_c = jnp.where(valid_c, s_c - slope * dist_c.astype(F32), NEG_INF)
            s_n = _bdot_nt(q, kn) * (B_HD ** -0.5)
            s_n = jnp.where(valid_n, s_n - slope * dist_n.astype(F32), NEG_INF)
            o = _sink_softmax_pv([(s_c, vc), (s_n, vn)], sink)
            for j in range(B_GROUP):
                h = g * B_GROUP + j
                o_ref[pl.ds(r0, ATT_S), pl.ds(h * B_HD, B_HD)] = o[j * ATT_S:(j + 1) * ATT_S].astype(o_ref.dtype)


def _attn_sample(q, kv, obuf, cache_k, cache_v, sinks, row0, n_batch):
    rows = ATT_BB * ATT_S
    blk0 = row0 // rows
    cache = pl.BlockSpec((ATT_BB, WINDOW, B_KV), lambda i, s: (i, 0, 0))
    return pl.pallas_call(
        _attn_sample_kernel,
        grid_spec=pltpu.PrefetchScalarGridSpec(
            num_scalar_prefetch=1,
            grid=(n_batch // ATT_BB,),
            in_specs=[pl.BlockSpec((rows, D_MODEL), lambda i, s: (blk0 + i, 0)),
                      pl.BlockSpec((rows, 2 * B_KV), lambda i, s: (blk0 + i, 0)),
                      cache, cache,
                      pl.BlockSpec(memory_space=pl.ANY)],
            out_specs=pl.BlockSpec((rows, D_MODEL), lambda i, s: (blk0 + i, 0)),
        ),
        out_shape=jax.ShapeDtypeStruct(obuf.shape, obuf.dtype),
        input_output_aliases={5: 0},
        compiler_params=_cparams("arbitrary"),
        name="attn_sample",
    )(sinks, q, kv, cache_k, cache_v, obuf)


def _split2(x):
    hi = x.astype(BF16)
    lo = (x - hi.astype(F32)).astype(BF16)
    return hi, lo


def _router_kernel(x_ref, w_ref, b_ref, idx_ref, gate_ref):
    xs = _split2(x_ref[...])
    ws = _split2(w_ref[...])
    logits = b_ref[...]
    for i, j in ((1, 1), (0, 1), (1, 0), (0, 0)):
        logits = logits + jnp.dot(xs[i], ws[j], preferred_element_type=F32)
    lane = lax.broadcasted_iota(jnp.int32, logits.shape, 1)
    real = lane < N_EXPERTS
    logits = jnp.where(real, logits, NEG_INF)
    m = jnp.max(logits, axis=-1, keepdims=True)
    e = jnp.where(real, jnp.exp(logits - m), 0.0)
    probs = e / jnp.sum(e, axis=-1, keepdims=True)
    p1 = jnp.max(probs, axis=-1, keepdims=True)
    i1 = jnp.min(jnp.where(probs == p1, lane, 128), axis=-1, keepdims=True)
    rest = jnp.where((lane == i1) | (~real), -1.0, probs)
    p2 = jnp.max(rest, axis=-1, keepdims=True)
    i2 = jnp.min(jnp.where(rest == p2, lane, 128), axis=-1, keepdims=True)
    tot = p1 + p2
    idx_ref[...] = jnp.where(lane == 0, i1, jnp.where(lane == 1, i2, 0))
    gate_ref[...] = jnp.where(lane == 0, p1 / tot, jnp.where(lane == 1, p2 / tot, 0.0))


def _router(x, w_router, b_router, tm):
    m, d = x.shape
    w = jnp.zeros((d, 128), F32).at[:, :N_EXPERTS].set(w_router)
    b = jnp.zeros((1, 128), F32).at[0, :N_EXPERTS].set(b_router)
    out = pl.BlockSpec((tm, 128), lambda i: (i, 0))
    return pl.pallas_call(
        _router_kernel,
        grid=(m // tm,),
        in_specs=[pl.BlockSpec((tm, d), lambda i: (i, 0)), pl.BlockSpec((d, 128), lambda i: (0, 0)),
                  pl.BlockSpec((1, 128), lambda i: (0, 0))],
        out_specs=[out, out],
        out_shape=[jax.ShapeDtypeStruct((m, 128), jnp.int32), jax.ShapeDtypeStruct((m, 128), F32)],
        compiler_params=_cparams("arbitrary"),
        name="moe_router",
    )(x, w, b)


def _moe_ffn_kernel(te_ref, xs_ref, wgu_ref, wd_ref, o_ref):
    del te_ref
    d_e = wd_ref.shape[1]
    gu = jnp.dot(xs_ref[...].astype(BF16), wgu_ref[0], preferred_element_type=F32)
    h = (_silu(gu[:, :d_e]) * gu[:, d_e:]).astype(BF16)
    o_ref[...] = jnp.dot(h, wd_ref[0], preferred_element_type=F32)


def _moe_ffn(tile_expert, xs, w_gu, w_down):
    a_pad, d = xs.shape
    d_e = w_down.shape[1]
    return pl.pallas_call(
        _moe_ffn_kernel,
        grid_spec=pltpu.PrefetchScalarGridSpec(
            num_scalar_prefetch=1,
            grid=(a_pad // MOE_TM,),
            in_specs=[pl.BlockSpec((MOE_TM, d), lambda i, te: (i, 0)),
                      pl.BlockSpec((1, d, 2 * d_e), lambda i, te: (te[i], 0, 0)),
                      pl.BlockSpec((1, d_e, d), lambda i, te: (te[i], 0, 0))],
            out_specs=pl.BlockSpec((MOE_TM, d), lambda i, te: (i, 0)),
        ),
        out_shape=jax.ShapeDtypeStruct((a_pad, d), F32),
        compiler_params=_cparams("arbitrary"),
        name="moe_experts",
    )(tile_expert, xs, w_gu, w_down)


def _moe_layer(x, xb, w_router, b_router, w_gu, w_down, ln_g, ln_b, tm):
    tt = x.shape[0]
    idx, gate = _router(x, w_router, b_router, tm)
    e_flat = idx[:, :2].reshape(-1)
    onehot = (e_flat[:, None] == jnp.arange(N_EXPERTS, dtype=jnp.int32)[None, :]).astype(jnp.int32)
    csum = jnp.cumsum(onehot, axis=0)
    counts = csum[-1]
    rank = jnp.sum((csum - onehot) * onehot, axis=1)
    padded = ((counts + MOE_TM - 1) // MOE_TM) * MOE_TM
    pend = jnp.cumsum(padded)
    pstart = pend - padded
    dest = pstart[e_flat] + rank
    a_pad = 2 * tt + N_EXPERTS * MOE_TM
    src_tok = jnp.zeros((a_pad,), jnp.int32).at[dest].set(jnp.arange(2 * tt, dtype=jnp.int32) // 2)
    tile_start = jnp.arange(a_pad // MOE_TM, dtype=jnp.int32) * MOE_TM
    tile_expert = jnp.minimum(jnp.sum(tile_start[:, None] >= pend[None, :], axis=1), N_EXPERTS - 1).astype(jnp.int32)
    del xb
    xs = jnp.take(x, src_tok, axis=0)
    ys = _moe_ffn(tile_expert, xs, w_gu, w_down)
    dest2 = dest.reshape(tt, 2)
    y0 = jnp.take(ys, dest2[:, 0], axis=0)
    y1 = jnp.take(ys, dest2[:, 1], axis=0)
    return _add2_res_ln(y0, y1, gate, x, ln_g, ln_b, tm)


def kernel(x_prompt, x_sample, state_gdn, state_conv, cache_win_k, cache_win_v, meta_tokens, a_w_in, a_conv_w, a_A_log, a_dt_bias, a_norm_w, a_w_out, b_w_kv, b_w_q, b_sinks, b_w_o, ln_mix_g, ln_mix_b, ln_ffn_g, ln_ffn_b, ffd_w_gu, ffd_w_down, moe_w_router, moe_b_router, moe_w_gu, moe_w_down):
    nb, seq, d = x_prompt.shape
    nsb, steps, _ = x_sample.shape
    lp = LEAD + N_META + seq
    n_prompt = nb * lp
    tt = n_prompt + nsb * steps
    tm = 384 if tt % 384 == 0 else 128
    d_ff = ffd_w_down.shape[1]

    pieces = []
    for b in range(nb):
        pieces += [jnp.zeros((LEAD, d), F32), meta_tokens.astype(F32), x_prompt[b]]
    x = jnp.concatenate(pieces + [x_sample.reshape(nsb * steps, d)], axis=0)
    xb = x.astype(BF16)

    p_gdn, p_conv, s_gdn, s_conv = [], [], [], []
    kv = None
    for l in range(DEPTH):
        if l < N_A_LAYERS:
            w_in = jnp.concatenate([a_w_in[l], jnp.zeros((d, A_PROJ_PAD - a_w_in.shape[2]), F32)], axis=1).astype(BF16)
            proj = _matmul(xb, w_in, F32, tm, A_PROJ_PAD // 3)
            alog_row = jnp.zeros((1, 128), F32).at[0, 8:16].set(a_A_log[l])
            dtb_row = jnp.zeros((1, 128), F32).at[0, 8:16].set(a_dt_bias[l])
            mix_in = jnp.zeros((tt, d), BF16)
            mix_in, s_p, c_p = _gdn_prompt(proj, mix_in, a_conv_w[l], alog_row, dtb_row, a_norm_w[l], nb, lp)
            mix_in, s_s, c_s = _gdn_sample(proj, mix_in, state_conv[l], state_gdn[l], a_conv_w[l], alog_row, dtb_row,
                                           a_norm_w[l], n_prompt, nsb)
            p_gdn.append(s_p)
            s_gdn.append(s_s)
            p_conv.append(c_p[:, 8 - (CONV_W - 1):])
            s_conv.append(c_s)
            w_mix = a_w_out[l].astype(BF16)
        else:
            j = l - N_A_LAYERS
            if j == 0:
                kv = _matmul(xb, b_w_kv.astype(BF16), F32, tm, 2 * B_KV)
            q = _matmul(xb, b_w_q[j].astype(BF16), BF16, tm, d)
            mix_in = jnp.zeros((tt, d), BF16)
            mix_in = _attn_prompt(q, kv, mix_in, b_sinks[j], nb, lp)
            mix_in = _attn_sample(q, kv, mix_in, cache_win_k.reshape(nsb, WINDOW, B_KV),
                                  cache_win_v.reshape(nsb, WINDOW, B_KV), b_sinks[j], n_prompt, nsb)
            w_mix = b_w_o[j].astype(BF16)
        x, xb = _matmul_res_ln(mix_in, w_mix, x, ln_mix_g[l], ln_mix_b[l], tm)
        if l % 2 == 0:
            hmid = _matmul_swiglu(xb, ffd_w_gu[l // 2].astype(BF16), d_ff, tm, d_ff // 2)
            x, xb = _matmul_res_ln(hmid, ffd_w_down[l // 2].astype(BF16), x, ln_ffn_g[l], ln_ffn_b[l], tm)
        else:
            x, xb = _moe_layer(x, xb, moe_w_router[l // 2], moe_b_router[l // 2], moe_w_gu[l // 2].astype(BF16),
                               moe_w_down[l // 2].astype(BF16), ln_ffn_g[l], ln_ffn_b[l], tm)

    y_prompt = x[:n_prompt].reshape(nb, lp, d)[:, LEAD + N_META:]
    y_sample = x[n_prompt:].reshape(nsb, steps, d)
    kvp = jnp.stack([kv[(b + 1) * lp - WINDOW:(b + 1) * lp] for b in range(nb)]).reshape(nb, WINDOW, 2, B_KV_HEADS, B_HD)
    kvs = kv[n_prompt:].reshape(nsb, steps, 2, B_KV_HEADS, B_HD)
    s_wk = jnp.concatenate([cache_win_k[:, steps:], kvs[:, :, 0]], axis=1)
    s_wv = jnp.concatenate([cache_win_v[:, steps:], kvs[:, :, 1]], axis=1)
    return (y_prompt, y_sample, jnp.stack(p_gdn), jnp.stack(p_conv), kvp[:, :, 0], kvp[:, :, 1],
            jnp.stack(s_gdn), jnp.stack(s_conv), s_wk, s_wv)
```

```python
import functools
import math

import jax
import jax.numpy as jnp
from jax import lax
from jax.experimental import pallas as pl
from jax.experimental.pallas import tpu as pltpu

F32 = jnp.float32
BF16 = jnp.bfloat16

D_MODEL = 1024
N_META = 16
A_HEADS = 8
A_DK = 128
A_DV = 128
CONV_W = 4
CONV_DIM = A_HEADS * (2 * A_DK + A_DV)
A_MAIN = CONV_DIM + A_HEADS * A_DV
A_PROJ_PAD = A_MAIN + 128
B_HD = 64
B_HEADS = 16
B_KV_HEADS = 4
B_GROUP = 4
B_KV = B_KV_HEADS * B_HD
WINDOW = 128
N_EXPERTS = 8
DEPTH = 4
N_A_LAYERS = 2
DEEPNORM_ALPHA = (2 * DEPTH) ** 0.25
LN_EPS = 1e-5
RMS_EPS = 1e-6
NEG_INF = -1e30

ROWS = 128
CHUNK = 64
INV_BASE = 32
LEAD = (-N_META) % ROWS
SAMPLE_BB = CHUNK // 8
MOE_TM = 256
VMEM_LIMIT = 56 * 1024 * 1024


def _cparams(*sem):
    return pltpu.CompilerParams(dimension_semantics=sem, vmem_limit_bytes=VMEM_LIMIT)


def _bdot(a, b):
    return jnp.dot(a.astype(BF16), b.astype(BF16), preferred_element_type=F32)


def _bdot_nt(a, b):
    return lax.dot_general(a.astype(BF16), b.astype(BF16), (((1,), (1,)), ((), ())), preferred_element_type=F32)


def _bdot_tn(a, b):
    return lax.dot_general(a.astype(BF16), b.astype(BF16), (((0,), (0,)), ((), ())), preferred_element_type=F32)


def _sigmoid(x):
    return 1.0 / (1.0 + jnp.exp(-x))


def _silu(x):
    return x * _sigmoid(x)


def _softplus(x):
    return jnp.maximum(x, 0.0) + jnp.log(1.0 + jnp.exp(-jnp.abs(x)))


def _layer_norm_rows(v, g, b):
    mu = jnp.mean(v, axis=-1, keepdims=True)
    vc = v - mu
    var = jnp.mean(vc * vc, axis=-1, keepdims=True)
    return vc * lax.rsqrt(var + LN_EPS) * g + b


def _mm_kernel(x_ref, w_ref, o_ref, wb_ref):
    @pl.when(pl.program_id(1) == 0)
    def _():
        wb_ref[...] = w_ref[...].astype(BF16)

    o_ref[...] = jnp.dot(x_ref[...], wb_ref[...], preferred_element_type=F32).astype(o_ref.dtype)


def _matmul(x, w, out_dtype, tm, tn):
    m, k = x.shape
    n = w.shape[1]
    return pl.pallas_call(
        _mm_kernel,
        grid=(n // tn, m // tm),
        in_specs=[pl.BlockSpec((tm, k), lambda j, i: (i, 0)), pl.BlockSpec((k, tn), lambda j, i: (0, j))],
        out_specs=pl.BlockSpec((tm, tn), lambda j, i: (i, j)),
        out_shape=jax.ShapeDtypeStruct((m, n), out_dtype),
        scratch_shapes=[pltpu.VMEM((k, tn), BF16)],
        compiler_params=_cparams("arbitrary", "arbitrary"),
        name="matmul",
    )(x, w)


def _swiglu_kernel(x_ref, wg_ref, wu_ref, o_ref, wgb_ref, wub_ref):
    @pl.when(pl.program_id(1) == 0)
    def _():
        wgb_ref[...] = wg_ref[...].astype(BF16)
        wub_ref[...] = wu_ref[...].astype(BF16)

    x = x_ref[...]
    g = jnp.dot(x, wgb_ref[...], preferred_element_type=F32)
    u = jnp.dot(x, wub_ref[...], preferred_element_type=F32)
    o_ref[...] = (_silu(g) * u).astype(o_ref.dtype)


def _matmul_swiglu(x, w_gu, d_ff, tm, tn):
    m, k = x.shape
    nt = d_ff // tn
    return pl.pallas_call(
        _swiglu_kernel,
        grid=(nt, m // tm),
        in_specs=[pl.BlockSpec((tm, k), lambda j, i: (i, 0)),
                  pl.BlockSpec((k, tn), lambda j, i: (0, j)),
                  pl.BlockSpec((k, tn), lambda j, i: (0, j + nt))],
        out_specs=pl.BlockSpec((tm, tn), lambda j, i: (i, j)),
        out_shape=jax.ShapeDtypeStruct((m, d_ff), BF16),
        scratch_shapes=[pltpu.VMEM((k, tn), BF16), pltpu.VMEM((k, tn), BF16)],
        compiler_params=_cparams("arbitrary", "arbitrary"),
        name="matmul_swiglu",
    )(x, w_gu, w_gu)


def _mm_res_ln_kernel(h_ref, w_ref, x_ref, g_ref, b_ref, o_ref, ob_ref, wb_ref):
    @pl.when(pl.program_id(0) == 0)
    def _():
        wb_ref[...] = w_ref[...].astype(BF16)

    f = jnp.dot(h_ref[...], wb_ref[...], preferred_element_type=F32)
    y = _layer_norm_rows(DEEPNORM_ALPHA * x_ref[...] + f, g_ref[...], b_ref[...])
    o_ref[...] = y
    ob_ref[...] = y.astype(BF16)


def _matmul_res_ln(h, w, x, g, b, tm):
    m, k = h.shape
    d = w.shape[1]
    return pl.pallas_call(
        _mm_res_ln_kernel,
        grid=(m // tm,),
        in_specs=[pl.BlockSpec((tm, k), lambda i: (i, 0)), pl.BlockSpec((k, d), lambda i: (0, 0)),
                  pl.BlockSpec((tm, d), lambda i: (i, 0)),
                  pl.BlockSpec((1, d), lambda i: (0, 0)), pl.BlockSpec((1, d), lambda i: (0, 0))],
        out_specs=[pl.BlockSpec((tm, d), lambda i: (i, 0)), pl.BlockSpec((tm, d), lambda i: (i, 0))],
        out_shape=[jax.ShapeDtypeStruct((m, d), F32), jax.ShapeDtypeStruct((m, d), BF16)],
        scratch_shapes=[pltpu.VMEM((k, d), BF16)],
        compiler_params=_cparams("arbitrary"),
        name="matmul_res_ln",
    )(h, w, x, g.reshape(1, d), b.reshape(1, d))


def _add2_res_ln_kernel(y0_ref, y1_ref, gate_ref, x_ref, g_ref, b_ref, o_ref, ob_ref):
    f = gate_ref[:, 0:1] * y0_ref[...] + gate_ref[:, 1:2] * y1_ref[...]
    y = _layer_norm_rows(DEEPNORM_ALPHA * x_ref[...] + f, g_ref[...], b_ref[...])
    o_ref[...] = y
    ob_ref[...] = y.astype(BF16)


def _add2_res_ln(y0, y1, gate, x, g, b, tm):
    m, d = x.shape
    row = pl.BlockSpec((tm, d), lambda i: (i, 0))
    vec = pl.BlockSpec((1, d), lambda i: (0, 0))
    return pl.pallas_call(
        _add2_res_ln_kernel,
        grid=(m // tm,),
        in_specs=[row, row, pl.BlockSpec((tm, 128), lambda i: (i, 0)), row, vec, vec],
        out_specs=[row, row],
        out_shape=[jax.ShapeDtypeStruct((m, d), F32), jax.ShapeDtypeStruct((m, d), BF16)],
        compiler_params=_cparams("arbitrary"),
        name="moe_combine_res_ln",
    )(y0, y1, gate, x, g.reshape(1, d), b.reshape(1, d))


def _gdn_gates(ba, alog_row, dtb_row, valid):
    beta = _sigmoid(ba)
    g = -jnp.exp(alog_row) * _softplus(ba + dtb_row)
    if valid is not None:
        beta = jnp.where(valid, beta, 0.0)
        g = jnp.where(valid, g, 0.0)
    return beta, g


def _group_cumsum(g, group):
    r_in = lax.broadcasted_iota(jnp.int32, g.shape, 0) % group
    s = 1
    while s < group:
        g = g + jnp.where(r_in >= s, pltpu.roll(g, s, axis=0), 0.0)
        s *= 2
    return g


def _group_last(g, group):
    rows = g.shape[0]
    r_in = lax.broadcasted_iota(jnp.int32, g.shape, 0) % group
    x = jnp.where(r_in == group - 1, g, 0.0)
    s = 1
    while s < group:
        x = x + jnp.where(r_in + s < group, pltpu.roll(x, rows - s, axis=0), 0.0)
        s *= 2
    return x


def _l2norm_rows(t):
    return t * lax.rsqrt(jnp.sum(t * t, axis=-1, keepdims=True) + 1e-6)


def _wy_block(q, k, v, beta_c, g_c, g_r, g_end_c, incl, strict, levels):
    c = q.shape[0]
    decay = jnp.where(incl, jnp.exp(jnp.where(incl, g_c - g_r, 0.0)), 0.0)
    kb = k * beta_c
    vb = v * beta_c
    a = jnp.where(strict, _bdot_nt(kb, k) * decay, 0.0)
    eye = (lax.broadcasted_iota(jnp.int32, (c, c), 0) == lax.broadcasted_iota(jnp.int32, (c, c), 1)).astype(F32)
    n = -a
    t = eye + n
    for _ in range(levels - 1):
        n = _bdot(n, n)
        t = t + _bdot(t, n)
    eg = jnp.exp(g_c)
    uw = _bdot(t, jnp.concatenate([vb, kb * eg], axis=1))
    u, w = uw[:, :A_DV], uw[:, A_DV:]
    qk = jnp.where(incl, _bdot_nt(q, k) * decay, 0.0)
    qg = q * eg
    kdec = k * jnp.exp(g_end_c - g_c)
    return u, w, qk, qg, kdec


def _gate_out(o, z, norm_w):
    o = o * lax.rsqrt(jnp.mean(o * o, axis=-1, keepdims=True) + RMS_EPS) * norm_w
    return o * _silu(z)


def _conv_head_inputs(xbuf_ref, cw_ref, row0, rows, h):
    outs = []
    for part in range(3):
        c0 = part * A_HEADS * A_DK + h * A_DK
        acc = xbuf_ref[pl.ds(row0 - 3, rows), pl.ds(c0, A_DK)] * cw_ref[0:1, pl.ds(c0, A_DK)]
        for j in range(1, CONV_W):
            acc = acc + xbuf_ref[pl.ds(row0 - 3 + j, rows), pl.ds(c0, A_DK)] * cw_ref[j:j + 1, pl.ds(c0, A_DK)]
        outs.append(_silu(acc))
    q = _l2norm_rows(outs[0]) * (A_DK ** -0.5)
    k = _l2norm_rows(outs[1])
    return q, k, outs[2]


def _blockdiag(a, b):
    z = jnp.zeros_like(a)
    return jnp.concatenate([jnp.concatenate([a, z], axis=1), jnp.concatenate([z, b], axis=1)], axis=0)


def _gdn_prompt_kernel(proj_ref, cw_ref, alog_ref, dtb_ref, nw_ref, obuf_ref, o_ref, s_out_ref, c_out_ref,
                       xbuf_ref, s_ref):
    del obuf_ref
    i = pl.program_id(1)

    @pl.when(i == 0)
    def _():
        xbuf_ref[0:8, :] = jnp.zeros((8, CONV_DIM), F32)
        s_ref[...] = jnp.zeros(s_ref.shape, F32)

    pos = i * ROWS + lax.broadcasted_iota(jnp.int32, (ROWS, 1), 0)
    valid = pos >= LEAD
    xbuf_ref[8:8 + ROWS, :] = jnp.where(valid, proj_ref[:, 0:CONV_DIM], 0.0)

    beta, g = _gdn_gates(proj_ref[:, A_MAIN:A_PROJ_PAD], alog_ref[...], dtb_ref[...], valid)
    gcum = _group_cumsum(g, ROWS)
    gcum_t = gcum.T

    ri = lax.broadcasted_iota(jnp.int32, (ROWS, ROWS), 0)
    ci = lax.broadcasted_iota(jnp.int32, (ROWS, ROWS), 1)
    incl = ri >= ci
    strict = ri > ci

    a_s, qk_s, rhs_s, qg_s, kdec_s, gend_s = [], [], [], [], [], []
    for h in range(A_HEADS):
        q, k, v = _conv_head_inputs(xbuf_ref, cw_ref, 8, ROWS, h)
        g_c = gcum[:, 8 + h:9 + h]
        g_r = gcum_t[8 + h:9 + h, :]
        g_end = gcum[ROWS - 1:ROWS, 8 + h:9 + h]
        decay = jnp.where(incl, jnp.exp(jnp.where(incl, g_c - g_r, 0.0)), 0.0)
        bcol = beta[:, h:h + 1]
        kb = k * bcol
        eg = jnp.exp(g_c)
        kq = _bdot_nt(jnp.concatenate([kb, q], axis=0), k)
        a_s.append(jnp.where(strict, kq[:ROWS] * decay, 0.0))
        qk_s.append(jnp.where(incl, kq[ROWS:] * decay, 0.0).astype(BF16))
        rhs_s.append(jnp.concatenate([v * bcol, kb * eg], axis=1).astype(BF16))
        qg_s.append((q * eg).astype(BF16))
        kdec_s.append((k * jnp.exp(g_end - g_c)).astype(BF16))
        gend_s.append(g_end)

    r2 = lax.broadcasted_iota(jnp.int32, (2 * ROWS, 2 * ROWS), 0)
    c2 = lax.broadcasted_iota(jnp.int32, (2 * ROWS, 2 * ROWS), 1)
    eye = (r2 == c2).astype(F32)
    pairs = A_HEADS // 2
    a_bd = [_blockdiag(a_s[2 * p], a_s[2 * p + 1]) for p in range(pairs)]
    base = (r2 // INV_BASE) == (c2 // INV_BASE)
    nbf = [jnp.where(base, -a, 0.0).astype(BF16) for a in a_bd]
    t32 = [eye + n.astype(F32) for n in nbf]
    size = 2
    while size < INV_BASE:
        nbf = [jnp.dot(n, n, preferred_element_type=F32).astype(BF16) for n in nbf]
        t32 = [t + jnp.dot(t.astype(BF16), n, preferred_element_type=F32) for t, n in zip(t32, nbf)]
        size *= 2
    while size < ROWS:
        off = ((r2 // (2 * size)) == (c2 // (2 * size))) & ((r2 // size) != (c2 // size))
        tbf = [t.astype(BF16) for t in t32]
        xs = [jnp.dot(jnp.where(off, a, 0.0).astype(BF16), t, preferred_element_type=F32).astype(BF16)
              for a, t in zip(a_bd, tbf)]
        t32 = [t - jnp.dot(tb, x, preferred_element_type=F32) for t, tb, x in zip(t32, tbf, xs)]
        size *= 2
    uw_s = [jnp.dot(t32[p].astype(BF16), jnp.concatenate([rhs_s[2 * p], rhs_s[2 * p + 1]], axis=0),
                    preferred_element_type=F32) for p in range(pairs)]

    for h in range(A_HEADS):
        uw = uw_s[h // 2][(h % 2) * ROWS:(h % 2 + 1) * ROWS]
        u, w = uw[:, :A_DV], uw[:, A_DV:]
        s = s_ref[h]
        ws = jnp.dot(jnp.concatenate([w.astype(BF16), qg_s[h]], axis=0), s.astype(BF16), preferred_element_type=F32)
        v_new = (u - ws[:ROWS]).astype(BF16)
        o = ws[ROWS:] + jnp.dot(qk_s[h], v_new, preferred_element_type=F32)
        s_ref[h] = s * jnp.exp(gend_s[h]) + lax.dot_general(kdec_s[h], v_new, (((0,), (0,)), ((), ())),
                                                            preferred_element_type=F32)
        z = proj_ref[:, pl.ds(CONV_DIM + h * A_DV, A_DV)]
        o_ref[:, pl.ds(h * A_DV, A_DV)] = _gate_out(o, z, nw_ref[...]).astype(o_ref.dtype)

    xbuf_ref[0:8, :] = xbuf_ref[ROWS:ROWS + 8, :]

    @pl.when(i == pl.num_programs(1) - 1)
    def _():
        s_out_ref[0] = s_ref[...]
        c_out_ref[0] = xbuf_ref[ROWS:ROWS + 8, :]


def _gdn_prompt(proj, obuf, conv_w, alog_row, dtb_row, norm_w, n_batch, lp):
    nblk = lp // ROWS
    vec = lambda n: pl.BlockSpec((1, n), lambda b, i: (0, 0))
    return pl.pallas_call(
        _gdn_prompt_kernel,
        grid=(n_batch, nblk),
        in_specs=[pl.BlockSpec((ROWS, A_PROJ_PAD), lambda b, i: (b * nblk + i, 0)),
                  pl.BlockSpec((CONV_W, CONV_DIM), lambda b, i: (0, 0)),
                  vec(128), vec(128), vec(A_DV),
                  pl.BlockSpec(memory_space=pl.ANY)],
        out_specs=[pl.BlockSpec((ROWS, D_MODEL), lambda b, i: (b * nblk + i, 0)),
                   pl.BlockSpec((1, A_HEADS, A_DK, A_DV), lambda b, i: (b, 0, 0, 0)),
                   pl.BlockSpec((1, 8, CONV_DIM), lambda b, i: (b, 0, 0))],
        out_shape=[jax.ShapeDtypeStruct(obuf.shape, obuf.dtype),
                   jax.ShapeDtypeStruct((n_batch, A_HEADS, A_DK, A_DV), F32),
                   jax.ShapeDtypeStruct((n_batch, 8, CONV_DIM), F32)],
        scratch_shapes=[pltpu.VMEM((ROWS + 8, CONV_DIM), F32), pltpu.VMEM((A_HEADS, A_DK, A_DV), F32)],
        input_output_aliases={5: 0},
        compiler_params=_cparams("arbitrary", "arbitrary"),
        name="gdn_prompt",
    )(proj, conv_w, alog_row, dtb_row, norm_w.reshape(1, A_DV), obuf)


def _gdn_sample_kernel(proj_ref, c0_ref, s0_ref, cw_ref, alog_ref, dtb_ref, nw_ref, obuf_ref, o_ref, s_out_ref,
                       c_out_ref, xbuf_ref):
    del obuf_ref
    nb, steps = SAMPLE_BB, CHUNK // SAMPLE_BB
    for b in range(nb):
        xbuf_ref[pl.ds(b * 16 + 5, 3), :] = c0_ref[b]
        xbuf_ref[pl.ds(b * 16 + 8, steps), :] = proj_ref[pl.ds(b * steps, steps), 0:CONV_DIM]
        c_out_ref[b] = proj_ref[pl.ds((b + 1) * steps - (CONV_W - 1), CONV_W - 1), 0:CONV_DIM]

    beta, g = _gdn_gates(proj_ref[:, A_MAIN:A_PROJ_PAD], alog_ref[...], dtb_ref[...], None)
    gcum = _group_cumsum(g, steps)
    gcum_t = gcum.T
    gend = _group_last(gcum, steps)

    ri = lax.broadcasted_iota(jnp.int32, (CHUNK, CHUNK), 0)
    ci = lax.broadcasted_iota(jnp.int32, (CHUNK, CHUNK), 1)
    same = (ri // steps) == (ci // steps)
    incl = same & (ri >= ci)
    strict = same & (ri > ci)

    for h in range(A_HEADS):
        qs, ks, vs = [], [], []
        for b in range(nb):
            qb, kb_, vb_ = _conv_head_inputs(xbuf_ref, cw_ref, b * 16 + 8, steps, h)
            qs.append(qb), ks.append(kb_), vs.append(vb_)
        q, k, v = (jnp.concatenate(t, axis=0) for t in (qs, ks, vs))
        g_c = gcum[:, 8 + h:9 + h]
        g_r = gcum_t[8 + h:9 + h, :]
        g_e = gend[:, 8 + h:9 + h]
        u, w, qk, qg, kdec = _wy_block(q, k, v, beta[:, h:h + 1], g_c, g_r, g_e, incl, strict, 3)
        v_news, o_s = [], []
        for b in range(nb):
            r0, r1 = b * steps, (b + 1) * steps
            s = s0_ref[b, h]
            ws = _bdot(jnp.concatenate([w[r0:r1], qg[r0:r1]], axis=0), s)
            v_new = u[r0:r1] - ws[:steps]
            v_news.append(v_new)
            o_s.append(ws[steps:])
            s_out_ref[b, h] = s * jnp.exp(g_e[r1 - 1:r1, :]) + _bdot_tn(kdec[r0:r1], v_new)
        o = jnp.concatenate(o_s, axis=0) + _bdot(qk, jnp.concatenate(v_news, axis=0))
        z = proj_ref[:, pl.ds(CONV_DIM + h * A_DV, A_DV)]
        o_ref[:, pl.ds(h * A_DV, A_DV)] = _gate_out(o, z, nw_ref[...]).astype(o_ref.dtype)


def _gdn_sample(proj, obuf, conv0, s0, conv_w, alog_row, dtb_row, norm_w, row0, n_batch):
    blk0 = row0 // CHUNK
    vec = lambda n: pl.BlockSpec((1, n), lambda i: (0, 0))
    st = pl.BlockSpec((SAMPLE_BB, A_HEADS, A_DK, A_DV), lambda i: (i, 0, 0, 0))
    cs = pl.BlockSpec((SAMPLE_BB, CONV_W - 1, CONV_DIM), lambda i: (i, 0, 0))
    return pl.pallas_call(
        _gdn_sample_kernel,
        grid=(n_batch // SAMPLE_BB,),
        in_specs=[pl.BlockSpec((CHUNK, A_PROJ_PAD), lambda i: (blk0 + i, 0)),
                  cs, st,
                  pl.BlockSpec((CONV_W, CONV_DIM), lambda i: (0, 0)),
                  vec(128), vec(128), vec(A_DV),
                  pl.BlockSpec(memory_space=pl.ANY)],
        out_specs=[pl.BlockSpec((CHUNK, D_MODEL), lambda i: (blk0 + i, 0)), st, cs],
        out_shape=[jax.ShapeDtypeStruct(obuf.shape, obuf.dtype),
                   jax.ShapeDtypeStruct(s0.shape, F32),
                   jax.ShapeDtypeStruct(conv0.shape, F32)],
        scratch_shapes=[pltpu.VMEM((SAMPLE_BB * 16, CONV_DIM), F32)],
        input_output_aliases={7: 0},
        compiler_params=_cparams("arbitrary"),
        name="gdn_sample",
    )(proj, conv0, s0, conv_w, alog_row, dtb_row, norm_w.reshape(1, A_DV), obuf)


def _alibi_slope(h):
    return 2.0 ** (-8.0 * (h + 1) / B_HEADS)


def _sink_softmax_pv(pieces, sink):
    m = sink
    for s, _ in pieces:
        m = jnp.maximum(m, jnp.max(s, axis=-1, keepdims=True))
    den = jnp.exp(sink - m)
    acc = None
    ps = []
    for s, _ in pieces:
        p = jnp.exp(s - m)
        den = den + jnp.sum(p, axis=-1, keepdims=True)
        ps.append(p)
    inv = 1.0 / den
    for p, (_, v) in zip(ps, pieces):
        t = _bdot(p * inv, v)
        acc = t if acc is None else acc + t
    return acc


def _attn_prompt_kernel(sink_ref, q_ref, kvp_ref, kvc_ref, obuf_ref, o_ref):
    del obuf_ref
    i = pl.program_id(1)
    r = lax.broadcasted_iota(jnp.int32, (ROWS, 2 * ROWS), 0)
    c = lax.broadcasted_iota(jnp.int32, (ROWS, 2 * ROWS), 1)
    dist = r - c + ROWS
    kpos = i * ROWS - ROWS - LEAD + c
    valid = (kpos >= 0) & (dist >= 0) & (dist <= WINDOW)
    dist_f = dist.astype(F32)
    for g in range(B_KV_HEADS):
        k = jnp.concatenate([kvp_ref[:, pl.ds(g * B_HD, B_HD)], kvc_ref[:, pl.ds(g * B_HD, B_HD)]], axis=0)
        v = jnp.concatenate([kvp_ref[:, pl.ds(B_KV + g * B_HD, B_HD)], kvc_ref[:, pl.ds(B_KV + g * B_HD, B_HD)]], axis=0)
        kb, vb = k.astype(BF16), v.astype(BF16)
        for j in range(B_GROUP):
            h = g * B_GROUP + j
            s = _bdot_nt(q_ref[:, pl.ds(h * B_HD, B_HD)], kb) * (B_HD ** -0.5)
            s = jnp.where(valid, s - _alibi_slope(h) * dist_f, NEG_INF)
            o = _sink_softmax_pv([(s, vb)], sink_ref[h])
            o_ref[:, pl.ds(h * B_HD, B_HD)] = o.astype(o_ref.dtype)


def _attn_prompt(q, kv, obuf, sinks, n_batch, lp):
    nblk = lp // ROWS
    return pl.pallas_call(
        _attn_prompt_kernel,
        grid_spec=pltpu.PrefetchScalarGridSpec(
            num_scalar_prefetch=1,
            grid=(n_batch, nblk),
            in_specs=[pl.BlockSpec((ROWS, D_MODEL), lambda b, i, s: (b * nblk + i, 0)),
                      pl.BlockSpec((ROWS, 2 * B_KV), lambda b, i, s: (b * nblk + jnp.maximum(i - 1, 0), 0)),
                      pl.BlockSpec((ROWS, 2 * B_KV), lambda b, i, s: (b * nblk + i, 0)),
                      pl.BlockSpec(memory_space=pl.ANY)],
            out_specs=pl.BlockSpec((ROWS, D_MODEL), lambda b, i, s: (b * nblk + i, 0)),
        ),
        out_shape=jax.ShapeDtypeStruct(obuf.shape, obuf.dtype),
        input_output_aliases={4: 0},
        compiler_params=_cparams("arbitrary", "arbitrary"),
        name="attn_prompt",
    )(sinks, q, kv, kv, obuf)


ATT_BB = 8
ATT_S = 8


def _attn_sample_kernel(sink_ref, q_ref, kvn_ref, ck_ref, cv_ref, obuf_ref, o_ref):
    del obuf_ref
    rows = B_GROUP * ATT_S
    t = lax.broadcasted_iota(jnp.int32, (rows, WINDOW), 0) % ATT_S
    c = lax.broadcasted_iota(jnp.int32, (rows, WINDOW), 1)
    dist_c = WINDOW + t - c
    valid_c = dist_c <= WINDOW
    tn = lax.broadcasted_iota(jnp.int32, (rows, ATT_S), 0) % ATT_S
    cn = lax.broadcasted_iota(jnp.int32, (rows, ATT_S), 1)
    dist_n = tn - cn
    valid_n = dist_n >= 0
    hrow = lax.broadcasted_iota(jnp.int32, (rows, 1), 0) // ATT_S
    for b in range(ATT_BB):
        r0 = b * ATT_S
        for g in range(B_KV_HEADS):
            slope = jnp.zeros((rows, 1), F32)
            sink = jnp.zeros((rows, 1), F32)
            for j in range(B_GROUP):
                slope = jnp.where(hrow == j, _alibi_slope(g * B_GROUP + j), slope)
                sink = jnp.where(hrow == j, sink_ref[g * B_GROUP + j], sink)
            q = jnp.concatenate([q_ref[pl.ds(r0, ATT_S), pl.ds((g * B_GROUP + j) * B_HD, B_HD)]
                                 for j in range(B_GROUP)], axis=0)
            kc = ck_ref[b, :, pl.ds(g * B_HD, B_HD)]
            vc = cv_ref[b, :, pl.ds(g * B_HD, B_HD)]
            kn = kvn_ref[pl.ds(r0, ATT_S), pl.ds(g * B_HD, B_HD)]
            vn = kvn_ref[pl.ds(r0, ATT_S), pl.ds(B_KV + g * B_HD, B_HD)]
            s_c = _bdot_nt(q, kc) * (B_HD ** -0.5)
            s_c = jnp.where(valid_c, s_c - slope * dist_c.astype(F32), NEG_INF)
            s_n = _bdot_nt(q, kn) * (B_HD ** -0.5)
            s_n = jnp.where(valid_n, s_n - slope * dist_n.astype(F32), NEG_INF)
            o = _sink_softmax_pv([(s_c, vc), (s_n, vn)], sink)
            for j in range(B_GROUP):
                h = g * B_GROUP + j
                o_ref[pl.ds(r0, ATT_S), pl.ds(h * B_HD, B_HD)] = o[j * ATT_S:(j + 1) * ATT_S].astype(o_ref.dtype)


def _attn_sample(q, kv, obuf, cache_k, cache_v, sinks, row0, n_batch):
    rows = ATT_BB * ATT_S
    blk0 = row0 // rows
    cache = pl.BlockSpec((ATT_BB, WINDOW, B_KV), lambda i, s: (i, 0, 0))
    return pl.pallas_call(
        _attn_sample_kernel,
        grid_spec=pltpu.PrefetchScalarGridSpec(
            num_scalar_prefetch=1,
            grid=(n_batch // ATT_BB,),
            in_specs=[pl.BlockSpec((rows, D_MODEL), lambda i, s: (blk0 + i, 0)),
                      pl.BlockSpec((rows, 2 * B_KV), lambda i, s: (blk0 + i, 0)),
                      cache, cache,
                      pl.BlockSpec(memory_space=pl.ANY)],
            out_specs=pl.BlockSpec((rows, D_MODEL), lambda i, s: (blk0 + i, 0)),
        ),
        out_shape=jax.ShapeDtypeStruct(obuf.shape, obuf.dtype),
        input_output_aliases={5: 0},
        compiler_params=_cparams("arbitrary"),
        name="attn_sample",
    )(sinks, q, kv, cache_k, cache_v, obuf)


def _split2(x):
    hi = x.astype(BF16)
    lo = (x - hi.astype(F32)).astype(BF16)
    return hi, lo


def _router_kernel(x_ref, w_ref, b_ref, idx_ref, gate_ref):
    xs = _split2(x_ref[...])
    ws = _split2(w_ref[...])
    logits = b_ref[...]
    for i, j in ((1, 1), (0, 1), (1, 0), (0, 0)):
        logits = logits + jnp.dot(xs[i], ws[j], preferred_element_type=F32)
    lane = lax.broadcasted_iota(jnp.int32, logits.shape, 1)
    real = lane < N_EXPERTS
    logits = jnp.where(real, logits, NEG_INF)
    m = jnp.max(logits, axis=-1, keepdims=True)
    e = jnp.where(real, jnp.exp(logits - m), 0.0)
    probs = e / jnp.sum(e, axis=-1, keepdims=True)
    p1 = jnp.max(probs, axis=-1, keepdims=True)
    i1 = jnp.min(jnp.where(probs == p1, lane, 128), axis=-1, keepdims=True)
    rest = jnp.where((lane == i1) | (~real), -1.0, probs)
    p2 = jnp.max(rest, axis=-1, keepdims=True)
    i2 = jnp.min(jnp.where(rest == p2, lane, 128), axis=-1, keepdims=True)
    tot = p1 + p2
    idx_ref[...] = jnp.where(lane == 0, i1, jnp.where(lane == 1, i2, 0))
    gate_ref[...] = jnp.where(lane == 0, p1 / tot, jnp.where(lane == 1, p2 / tot, 0.0))


def _router(x, w_router, b_router, tm):
    m, d = x.shape
    w = jnp.zeros((d, 128), F32).at[:, :N_EXPERTS].set(w_router)
    b = jnp.zeros((1, 128), F32).at[0, :N_EXPERTS].set(b_router)
    out = pl.BlockSpec((tm, 128), lambda i: (i, 0))
    return pl.pallas_call(
        _router_kernel,
        grid=(m // tm,),
        in_specs=[pl.BlockSpec((tm, d), lambda i: (i, 0)), pl.BlockSpec((d, 128), lambda i: (0, 0)),
                  pl.BlockSpec((1, 128), lambda i: (0, 0))],
        out_specs=[out, out],
        out_shape=[jax.ShapeDtypeStruct((m, 128), jnp.int32), jax.ShapeDtypeStruct((m, 128), F32)],
        compiler_params=_cparams("arbitrary"),
        name="moe_router",
    )(x, w, b)


def _moe_ffn_kernel(te_ref, xs_ref, wgu_ref, wd_ref, o_ref, wgub_ref, wdb_ref):
    i = pl.program_id(0)

    @pl.when((i == 0) | (te_ref[i] != te_ref[jnp.maximum(i - 1, 0)]))
    def _():
        wgub_ref[...] = wgu_ref[0].astype(BF16)
        wdb_ref[...] = wd_ref[0].astype(BF16)

    d_e = wd_ref.shape[1]
    gu = jnp.dot(xs_ref[...].astype(BF16), wgub_ref[...], preferred_element_type=F32)
    h = (_silu(gu[:, :d_e]) * gu[:, d_e:]).astype(BF16)
    o_ref[...] = jnp.dot(h, wdb_ref[...], preferred_element_type=F32)


def _moe_ffn(tile_expert, xs, w_gu, w_down):
    a_pad, d = xs.shape
    d_e = w_down.shape[1]
    return pl.pallas_call(
        _moe_ffn_kernel,
        grid_spec=pltpu.PrefetchScalarGridSpec(
            num_scalar_prefetch=1,
            grid=(a_pad // MOE_TM,),
            in_specs=[pl.BlockSpec((MOE_TM, d), lambda i, te: (i, 0)),
                      pl.BlockSpec((1, d, 2 * d_e), lambda i, te: (te[i], 0, 0)),
                      pl.BlockSpec((1, d_e, d), lambda i, te: (te[i], 0, 0))],
            out_specs=pl.BlockSpec((MOE_TM, d), lambda i, te: (i, 0)),
            scratch_shapes=[pltpu.VMEM((d, 2 * d_e), BF16), pltpu.VMEM((d_e, d), BF16)],
        ),
        out_shape=jax.ShapeDtypeStruct((a_pad, d), F32),
        compiler_params=_cparams("arbitrary"),
        name="moe_experts",
    )(tile_expert, xs, w_gu, w_down)


def _moe_layer(x, xb, w_router, b_router, w_gu, w_down, layer, ln_g, ln_b, tm):
    tt = x.shape[0]
    idx, gate = _router(x, w_router, b_router, tm)
    e_flat = idx[:, :2].reshape(-1)
    onehot = (e_flat[:, None] == jnp.arange(N_EXPERTS, dtype=jnp.int32)[None, :]).astype(jnp.int32)
    csum = jnp.cumsum(onehot, axis=0)
    counts = csum[-1]
    rank = jnp.sum((csum - onehot) * onehot, axis=1)
    padded = ((counts + MOE_TM - 1) // MOE_TM) * MOE_TM
    pend = jnp.cumsum(padded)
    pstart = pend - padded
    dest = pstart[e_flat] + rank
    a_pad = 2 * tt + N_EXPERTS * MOE_TM
    src_tok = jnp.zeros((a_pad,), jnp.int32).at[dest].set(jnp.arange(2 * tt, dtype=jnp.int32) // 2)
    tile_start = jnp.arange(a_pad // MOE_TM, dtype=jnp.int32) * MOE_TM
    tile_expert = jnp.minimum(jnp.sum(tile_start[:, None] >= pend[None, :], axis=1), N_EXPERTS - 1).astype(jnp.int32)
    del xb
    xs = jnp.take(x, src_tok, axis=0)
    ys = _moe_ffn(tile_expert + layer * N_EXPERTS, xs, w_gu, w_down)
    dest2 = dest.reshape(tt, 2)
    y0 = jnp.take(ys, dest2[:, 0], axis=0)
    y1 = jnp.take(ys, dest2[:, 1], axis=0)
    return _add2_res_ln(y0, y1, gate, x, ln_g, ln_b, tm)


def kernel(x_prompt, x_sample, state_gdn, state_conv, cache_win_k, cache_win_v, meta_tokens, a_w_in, a_conv_w, a_A_log, a_dt_bias, a_norm_w, a_w_out, b_w_kv, b_w_q, b_sinks, b_w_o, ln_mix_g, ln_mix_b, ln_ffn_g, ln_ffn_b, ffd_w_gu, ffd_w_down, moe_w_router, moe_b_router, moe_w_gu, moe_w_down):
    nb, seq, d = x_prompt.shape
    nsb, steps, _ = x_sample.shape
    lp = LEAD + N_META + seq
    n_prompt = nb * lp
    tt = n_prompt + nsb * steps
    tm = 384 if tt % 384 == 0 else 128
    d_ff = ffd_w_down.shape[1]

    pieces = []
    for b in range(nb):
        pieces += [jnp.zeros((LEAD, d), F32), meta_tokens.astype(F32), x_prompt[b]]
    x = jnp.concatenate(pieces + [x_sample.reshape(nsb * steps, d)], axis=0)
    xb = x.astype(BF16)

    p_gdn, p_conv, s_gdn, s_conv = [], [], [], []
    kv = None
    n_moe, n_exp = moe_w_gu.shape[:2]
    moe_gu_all = moe_w_gu.reshape(n_moe * n_exp, *moe_w_gu.shape[2:])
    moe_down_all = moe_w_down.reshape(n_moe * n_exp, *moe_w_down.shape[2:])
    mix_in = jnp.zeros((tt, d), BF16)
    for l in range(DEPTH):
        if l < N_A_LAYERS:
            w_in = jnp.concatenate([a_w_in[l], jnp.zeros((d, A_PROJ_PAD - a_w_in.shape[2]), F32)], axis=1)
            proj = _matmul(xb, w_in, F32, tm, A_PROJ_PAD // 3)
            alog_row = jnp.zeros((1, 128), F32).at[0, 8:16].set(a_A_log[l])
            dtb_row = jnp.zeros((1, 128), F32).at[0, 8:16].set(a_dt_bias[l])
            mix_in, s_p, c_p = _gdn_prompt(proj, mix_in, a_conv_w[l], alog_row, dtb_row, a_norm_w[l], nb, lp)
            mix_in, s_s, c_s = _gdn_sample(proj, mix_in, state_conv[l], state_gdn[l], a_conv_w[l], alog_row, dtb_row,
                                           a_norm_w[l], n_prompt, nsb)
            p_gdn.append(s_p)
            s_gdn.append(s_s)
            p_conv.append(c_p[:, 8 - (CONV_W - 1):])
            s_conv.append(c_s)
            w_mix = a_w_out[l]
        else:
            j = l - N_A_LAYERS
            if j == 0:
                kv = _matmul(xb, b_w_kv, F32, tm, 2 * B_KV)
            q = _matmul(xb, b_w_q[j], BF16, tm, d)
            mix_in = _attn_prompt(q, kv, mix_in, b_sinks[j], nb, lp)
            mix_in = _attn_sample(q, kv, mix_in, cache_win_k.reshape(nsb, WINDOW, B_KV),
                                  cache_win_v.reshape(nsb, WINDOW, B_KV), b_sinks[j], n_prompt, nsb)
            w_mix = b_w_o[j]
        x, xb = _matmul_res_ln(mix_in, w_mix, x, ln_mix_g[l], ln_mix_b[l], tm)
        if l % 2 == 0:
            hmid = _matmul_swiglu(xb, ffd_w_gu[l // 2], d_ff, tm, d_ff // 2)
            x, xb = _matmul_res_ln(hmid, ffd_w_down[l // 2], x, ln_ffn_g[l], ln_ffn_b[l], tm)
        else:
            x, xb = _moe_layer(x, xb, moe_w_router[l // 2], moe_b_router[l // 2], moe_gu_all, moe_down_all, l // 2,
                               ln_ffn_g[l], ln_ffn_b[l], tm)

    y_prompt = x[:n_prompt].reshape(nb, lp, d)[:, LEAD + N_META:]
    y_sample = x[n_prompt:].reshape(nsb, steps, d)
    kvp = jnp.stack([kv[(b + 1) * lp - WINDOW:(b + 1) * lp] for b in range(nb)]).reshape(nb, WINDOW, 2, B_KV_HEADS, B_HD)
    kvs = kv[n_prompt:].reshape(nsb, steps, 2, B_KV_HEADS, B_HD)
    s_wk = jnp.concatenate([cache_win_k[:, steps:], kvs[:, :, 0]], axis=1)
    s_wv = jnp.concatenate([cache_win_v[:, steps:], kvs[:, :, 1]], axis=1)
    return (y_prompt, y_sample, jnp.stack(p_gdn), jnp.stack(p_conv), kvp[:, :, 0], kvp[:, :, 1],
            jnp.stack(s_gdn), jnp.stack(s_conv), s_wk, s_wv)
```

```python
import functools
import math

import jax
import jax.numpy as jnp
from jax import lax
from jax.experimental import pallas as pl
from jax.experimental.pallas import tpu as pltpu

F32 = jnp.float32
BF16 = jnp.bfloat16

D_MODEL = 1024
N_META = 16
A_HEADS = 8
A_DK = 128
A_DV = 128
CONV_W = 4
CONV_DIM = A_HEADS * (2 * A_DK + A_DV)
A_MAIN = CONV_DIM + A_HEADS * A_DV
A_PROJ_PAD = A_MAIN + 128
B_HD = 64
B_HEADS = 16
B_KV_HEADS = 4
B_GROUP = 4
B_KV = B_KV_HEADS * B_HD
WINDOW = 128
N_EXPERTS = 8
DEPTH = 4
N_A_LAYERS = 2
DEEPNORM_ALPHA = (2 * DEPTH) ** 0.25
LN_EPS = 1e-5
RMS_EPS = 1e-6
NEG_INF = -1e30

ROWS = 128
CHUNK = 64
INV_BASE = 32
LEAD = (-N_META) % ROWS
SAMPLE_BB = CHUNK // 8
MOE_TM = 256
VMEM_LIMIT = 56 * 1024 * 1024


def _cparams(*sem):
    return pltpu.CompilerParams(dimension_semantics=sem, vmem_limit_bytes=VMEM_LIMIT)


def _bdot(a, b):
    return jnp.dot(a.astype(BF16), b.astype(BF16), preferred_element_type=F32)


def _bdot_nt(a, b):
    return lax.dot_general(a.astype(BF16), b.astype(BF16), (((1,), (1,)), ((), ())), preferred_element_type=F32)


def _bdot_tn(a, b):
    return lax.dot_general(a.astype(BF16), b.astype(BF16), (((0,), (0,)), ((), ())), preferred_element_type=F32)


def _sigmoid(x):
    return 1.0 / (1.0 + jnp.exp(-x))


def _silu(x):
    return x * _sigmoid(x)


def _softplus(x):
    return jnp.maximum(x, 0.0) + jnp.log(1.0 + jnp.exp(-jnp.abs(x)))


def _layer_norm_rows(v, g, b):
    mu = jnp.mean(v, axis=-1, keepdims=True)
    vc = v - mu
    var = jnp.mean(vc * vc, axis=-1, keepdims=True)
    return vc * lax.rsqrt(var + LN_EPS) * g + b


def _mm_kernel(x_ref, w_ref, o_ref, wb_ref):
    @pl.when(pl.program_id(1) == 0)
    def _():
        wb_ref[...] = w_ref[...].astype(BF16)

    o_ref[...] = jnp.dot(x_ref[...], wb_ref[...], preferred_element_type=F32).astype(o_ref.dtype)


def _matmul(x, w, out_dtype, tm, tn):
    m, k = x.shape
    n = w.shape[1]
    return pl.pallas_call(
        _mm_kernel,
        grid=(n // tn, m // tm),
        in_specs=[pl.BlockSpec((tm, k), lambda j, i: (i, 0)), pl.BlockSpec((k, tn), lambda j, i: (0, j))],
        out_specs=pl.BlockSpec((tm, tn), lambda j, i: (i, j)),
        out_shape=jax.ShapeDtypeStruct((m, n), out_dtype),
        scratch_shapes=[pltpu.VMEM((k, tn), BF16)],
        compiler_params=_cparams("arbitrary", "arbitrary"),
        name="matmul",
    )(x, w)


def _swiglu_kernel(x_ref, wg_ref, wu_ref, o_ref, wgb_ref, wub_ref):
    @pl.when(pl.program_id(1) == 0)
    def _():
        wgb_ref[...] = wg_ref[...].astype(BF16)
        wub_ref[...] = wu_ref[...].astype(BF16)

    x = x_ref[...]
    g = jnp.dot(x, wgb_ref[...], preferred_element_type=F32)
    u = jnp.dot(x, wub_ref[...], preferred_element_type=F32)
    o_ref[...] = (_silu(g) * u).astype(o_ref.dtype)


def _matmul_swiglu(x, w_gu, d_ff, tm, tn):
    m, k = x.shape
    nt = d_ff // tn
    return pl.pallas_call(
        _swiglu_kernel,
        grid=(nt, m // tm),
        in_specs=[pl.BlockSpec((tm, k), lambda j, i: (i, 0)),
                  pl.BlockSpec((k, tn), lambda j, i: (0, j)),
                  pl.BlockSpec((k, tn), lambda j, i: (0, j + nt))],
        out_specs=pl.BlockSpec((tm, tn), lambda j, i: (i, j)),
        out_shape=jax.ShapeDtypeStruct((m, d_ff), BF16),
        scratch_shapes=[pltpu.VMEM((k, tn), BF16), pltpu.VMEM((k, tn), BF16)],
        compiler_params=_cparams("arbitrary", "arbitrary"),
        name="matmul_swiglu",
    )(x, w_gu, w_gu)


def _mm_res_ln_kernel(h_ref, w_ref, x_ref, g_ref, b_ref, o_ref, ob_ref, wb_ref):
    @pl.when(pl.program_id(0) == 0)
    def _():
        wb_ref[...] = w_ref[...].astype(BF16)

    f = jnp.dot(h_ref[...], wb_ref[...], preferred_element_type=F32)
    y = _layer_norm_rows(DEEPNORM_ALPHA * x_ref[...] + f, g_ref[...], b_ref[...])
    o_ref[...] = y
    ob_ref[...] = y.astype(BF16)


def _matmul_res_ln(h, w, x, g, b, tm):
    m, k = h.shape
    d = w.shape[1]
    return pl.pallas_call(
        _mm_res_ln_kernel,
        grid=(m // tm,),
        in_specs=[pl.BlockSpec((tm, k), lambda i: (i, 0)), pl.BlockSpec((k, d), lambda i: (0, 0)),
                  pl.BlockSpec((tm, d), lambda i: (i, 0)),
                  pl.BlockSpec((1, d), lambda i: (0, 0)), pl.BlockSpec((1, d), lambda i: (0, 0))],
        out_specs=[pl.BlockSpec((tm, d), lambda i: (i, 0)), pl.BlockSpec((tm, d), lambda i: (i, 0))],
        out_shape=[jax.ShapeDtypeStruct((m, d), F32), jax.ShapeDtypeStruct((m, d), BF16)],
        scratch_shapes=[pltpu.VMEM((k, d), BF16)],
        compiler_params=_cparams("arbitrary"),
        name="matmul_res_ln",
    )(h, w, x, g.reshape(1, d), b.reshape(1, d))


def _add2_res_ln_kernel(y0_ref, y1_ref, gate_ref, x_ref, g_ref, b_ref, o_ref, ob_ref):
    f = gate_ref[:, 0:1] * y0_ref[...] + gate_ref[:, 1:2] * y1_ref[...]
    y = _layer_norm_rows(DEEPNORM_ALPHA * x_ref[...] + f, g_ref[...], b_ref[...])
    o_ref[...] = y
    ob_ref[...] = y.astype(BF16)


def _add2_res_ln(y0, y1, gate, x, g, b, tm):
    m, d = x.shape
    row = pl.BlockSpec((tm, d), lambda i: (i, 0))
    vec = pl.BlockSpec((1, d), lambda i: (0, 0))
    return pl.pallas_call(
        _add2_res_ln_kernel,
        grid=(m // tm,),
        in_specs=[row, row, pl.BlockSpec((tm, 128), lambda i: (i, 0)), row, vec, vec],
        out_specs=[row, row],
        out_shape=[jax.ShapeDtypeStruct((m, d), F32), jax.ShapeDtypeStruct((m, d), BF16)],
        compiler_params=_cparams("arbitrary"),
        name="moe_combine_res_ln",
    )(y0, y1, gate, x, g.reshape(1, d), b.reshape(1, d))


def _gdn_gates(ba, alog_row, dtb_row, valid):
    beta = _sigmoid(ba)
    g = -jnp.exp(alog_row) * _softplus(ba + dtb_row)
    if valid is not None:
        beta = jnp.where(valid, beta, 0.0)
        g = jnp.where(valid, g, 0.0)
    return beta, g


def _group_cumsum(g, group):
    r_in = lax.broadcasted_iota(jnp.int32, g.shape, 0) % group
    s = 1
    while s < group:
        g = g + jnp.where(r_in >= s, pltpu.roll(g, s, axis=0), 0.0)
        s *= 2
    return g


def _group_last(g, group):
    rows = g.shape[0]
    r_in = lax.broadcasted_iota(jnp.int32, g.shape, 0) % group
    x = jnp.where(r_in == group - 1, g, 0.0)
    s = 1
    while s < group:
        x = x + jnp.where(r_in + s < group, pltpu.roll(x, rows - s, axis=0), 0.0)
        s *= 2
    return x


def _l2norm_rows(t):
    return t * lax.rsqrt(jnp.sum(t * t, axis=-1, keepdims=True) + 1e-6)


def _wy_block(q, k, v, beta_c, g_c, g_r, g_end_c, incl, strict, levels):
    c = q.shape[0]
    decay = jnp.where(incl, jnp.exp(jnp.where(incl, g_c - g_r, 0.0)), 0.0)
    kb = k * beta_c
    vb = v * beta_c
    a = jnp.where(strict, _bdot_nt(kb, k) * decay, 0.0)
    eye = (lax.broadcasted_iota(jnp.int32, (c, c), 0) == lax.broadcasted_iota(jnp.int32, (c, c), 1)).astype(F32)
    n = -a
    t = eye + n
    for _ in range(levels - 1):
        n = _bdot(n, n)
        t = t + _bdot(t, n)
    eg = jnp.exp(g_c)
    uw = _bdot(t, jnp.concatenate([vb, kb * eg], axis=1))
    u, w = uw[:, :A_DV], uw[:, A_DV:]
    qk = jnp.where(incl, _bdot_nt(q, k) * decay, 0.0)
    qg = q * eg
    kdec = k * jnp.exp(g_end_c - g_c)
    return u, w, qk, qg, kdec


def _gate_out(o, z, norm_w):
    o = o * lax.rsqrt(jnp.mean(o * o, axis=-1, keepdims=True) + RMS_EPS) * norm_w
    return o * _silu(z)


def _conv_head_inputs(xbuf_ref, cw_ref, row0, rows, h):
    outs = []
    for part in range(3):
        c0 = part * A_HEADS * A_DK + h * A_DK
        acc = xbuf_ref[pl.ds(row0 - 3, rows), pl.ds(c0, A_DK)] * cw_ref[0:1, pl.ds(c0, A_DK)]
        for j in range(1, CONV_W):
            acc = acc + xbuf_ref[pl.ds(row0 - 3 + j, rows), pl.ds(c0, A_DK)] * cw_ref[j:j + 1, pl.ds(c0, A_DK)]
        outs.append(_silu(acc))
    q = _l2norm_rows(outs[0]) * (A_DK ** -0.5)
    k = _l2norm_rows(outs[1])
    return q, k, outs[2]


def _blockdiag(a, b):
    z = jnp.zeros_like(a)
    return jnp.concatenate([jnp.concatenate([a, z], axis=1), jnp.concatenate([z, b], axis=1)], axis=0)


def _gdn_prompt_kernel(proj_ref, cw_ref, alog_ref, dtb_ref, nw_ref, obuf_ref, o_ref, s_out_ref, c_out_ref,
                       xbuf_ref, s_ref):
    del obuf_ref
    i = pl.program_id(1)

    @pl.when(i == 0)
    def _():
        xbuf_ref[0:8, :] = jnp.zeros((8, CONV_DIM), F32)
        s_ref[...] = jnp.zeros(s_ref.shape, F32)

    pos = i * ROWS + lax.broadcasted_iota(jnp.int32, (ROWS, 1), 0)
    valid = pos >= LEAD
    xbuf_ref[8:8 + ROWS, :] = jnp.where(valid, proj_ref[:, 0:CONV_DIM], 0.0)

    beta, g = _gdn_gates(proj_ref[:, A_MAIN:A_PROJ_PAD], alog_ref[...], dtb_ref[...], valid)
    gcum = _group_cumsum(g, ROWS)
    gcum_t = gcum.T

    ri = lax.broadcasted_iota(jnp.int32, (ROWS, ROWS), 0)
    ci = lax.broadcasted_iota(jnp.int32, (ROWS, ROWS), 1)
    incl = ri >= ci
    strict = ri > ci

    a_s, qk_s, rhs_s, qg_s, kdec_s, gend_s = [], [], [], [], [], []
    for h in range(A_HEADS):
        q, k, v = _conv_head_inputs(xbuf_ref, cw_ref, 8, ROWS, h)
        g_c = gcum[:, 8 + h:9 + h]
        g_r = gcum_t[8 + h:9 + h, :]
        g_end = gcum[ROWS - 1:ROWS, 8 + h:9 + h]
        decay = jnp.where(incl, jnp.exp(jnp.where(incl, g_c - g_r, 0.0)), 0.0)
        bcol = beta[:, h:h + 1]
        kb = k * bcol
        eg = jnp.exp(g_c)
        kq = _bdot_nt(jnp.concatenate([kb, q], axis=0), k)
        a_s.append(jnp.where(strict, kq[:ROWS] * decay, 0.0))
        qk_s.append(jnp.where(incl, kq[ROWS:] * decay, 0.0).astype(BF16))
        rhs_s.append(jnp.concatenate([v * bcol, kb * eg], axis=1).astype(BF16))
        qg_s.append((q * eg).astype(BF16))
        kdec_s.append((k * jnp.exp(g_end - g_c)).astype(BF16))
        gend_s.append(g_end)

    r2 = lax.broadcasted_iota(jnp.int32, (2 * ROWS, 2 * ROWS), 0)
    c2 = lax.broadcasted_iota(jnp.int32, (2 * ROWS, 2 * ROWS), 1)
    eye = (r2 == c2).astype(F32)
    pairs = A_HEADS // 2
    a_bd = [_blockdiag(a_s[2 * p], a_s[2 * p + 1]) for p in range(pairs)]
    base = (r2 // INV_BASE) == (c2 // INV_BASE)
    nbf = [jnp.where(base, -a, 0.0).astype(BF16) for a in a_bd]
    t32 = [eye + n.astype(F32) for n in nbf]
    size = 2
    while size < INV_BASE:
        nbf = [jnp.dot(n, n, preferred_element_type=F32).astype(BF16) for n in nbf]
        t32 = [t + jnp.dot(t.astype(BF16), n, preferred_element_type=F32) for t, n in zip(t32, nbf)]
        size *= 2
    while size < ROWS:
        off = ((r2 // (2 * size)) == (c2 // (2 * size))) & ((r2 // size) != (c2 // size))
        tbf = [t.astype(BF16) for t in t32]
        xs = [jnp.dot(jnp.where(off, a, 0.0).astype(BF16), t, preferred_element_type=F32).astype(BF16)
              for a, t in zip(a_bd, tbf)]
        t32 = [t - jnp.dot(tb, x, preferred_element_type=F32) for t, tb, x in zip(t32, tbf, xs)]
        size *= 2
    uw_s = [jnp.dot(t32[p].astype(BF16), jnp.concatenate([rhs_s[2 * p], rhs_s[2 * p + 1]], axis=0),
                    preferred_element_type=F32) for p in range(pairs)]

    for h in range(A_HEADS):
        uw = uw_s[h // 2][(h % 2) * ROWS:(h % 2 + 1) * ROWS]
        u, w = uw[:, :A_DV], uw[:, A_DV:]
        s = s_ref[h]
        ws = jnp.dot(jnp.concatenate([w.astype(BF16), qg_s[h]], axis=0), s.astype(BF16), preferred_element_type=F32)
        v_new = (u - ws[:ROWS]).astype(BF16)
        o = ws[ROWS:] + jnp.dot(qk_s[h], v_new, preferred_element_type=F32)
        s_ref[h] = s * jnp.exp(gend_s[h]) + lax.dot_general(kdec_s[h], v_new, (((0,), (0,)), ((), ())),
                                                            preferred_element_type=F32)
        z = proj_ref[:, pl.ds(CONV_DIM + h * A_DV, A_DV)]
        o_ref[:, pl.ds(h * A_DV, A_DV)] = _gate_out(o, z, nw_ref[...]).astype(o_ref.dtype)

    xbuf_ref[0:8, :] = xbuf_ref[ROWS:ROWS + 8, :]

    @pl.when(i == pl.num_programs(1) - 1)
    def _():
        s_out_ref[0] = s_ref[...]
        c_out_ref[0] = xbuf_ref[ROWS:ROWS + 8, :]


def _gdn_prompt(proj, obuf, conv_w, alog_row, dtb_row, norm_w, n_batch, lp):
    nblk = lp // ROWS
    vec = lambda n: pl.BlockSpec((1, n), lambda b, i: (0, 0))
    return pl.pallas_call(
        _gdn_prompt_kernel,
        grid=(n_batch, nblk),
        in_specs=[pl.BlockSpec((ROWS, A_PROJ_PAD), lambda b, i: (b * nblk + i, 0)),
                  pl.BlockSpec((CONV_W, CONV_DIM), lambda b, i: (0, 0)),
                  vec(128), vec(128), vec(A_DV),
                  pl.BlockSpec(memory_space=pl.ANY)],
        out_specs=[pl.BlockSpec((ROWS, D_MODEL), lambda b, i: (b * nblk + i, 0)),
                   pl.BlockSpec((1, A_HEADS, A_DK, A_DV), lambda b, i: (b, 0, 0, 0)),
                   pl.BlockSpec((1, 8, CONV_DIM), lambda b, i: (b, 0, 0))],
        out_shape=[jax.ShapeDtypeStruct(obuf.shape, obuf.dtype),
                   jax.ShapeDtypeStruct((n_batch, A_HEADS, A_DK, A_DV), F32),
                   jax.ShapeDtypeStruct((n_batch, 8, CONV_DIM), F32)],
        scratch_shapes=[pltpu.VMEM((ROWS + 8, CONV_DIM), F32), pltpu.VMEM((A_HEADS, A_DK, A_DV), F32)],
        input_output_aliases={5: 0},
        compiler_params=_cparams("arbitrary", "arbitrary"),
        name="gdn_prompt",
    )(proj, conv_w, alog_row, dtb_row, norm_w.reshape(1, A_DV), obuf)


def _gdn_sample_kernel(proj_ref, c0_ref, s0_ref, cw_ref, alog_ref, dtb_ref, nw_ref, obuf_ref, o_ref, s_out_ref,
                       c_out_ref, xbuf_ref):
    del obuf_ref
    nb, steps = SAMPLE_BB, CHUNK // SAMPLE_BB
    for b in range(nb):
        xbuf_ref[pl.ds(b * 16 + 5, 3), :] = c0_ref[b]
        xbuf_ref[pl.ds(b * 16 + 8, steps), :] = proj_ref[pl.ds(b * steps, steps), 0:CONV_DIM]
        c_out_ref[b] = proj_ref[pl.ds((b + 1) * steps - (CONV_W - 1), CONV_W - 1), 0:CONV_DIM]

    beta, g = _gdn_gates(proj_ref[:, A_MAIN:A_PROJ_PAD], alog_ref[...], dtb_ref[...], None)
    gcum = _group_cumsum(g, steps)
    gcum_t = gcum.T
    gend = _group_last(gcum, steps)

    ri = lax.broadcasted_iota(jnp.int32, (CHUNK, CHUNK), 0)
    ci = lax.broadcasted_iota(jnp.int32, (CHUNK, CHUNK), 1)
    same = (ri // steps) == (ci // steps)
    incl = same & (ri >= ci)
    strict = same & (ri > ci)

    for h in range(A_HEADS):
        qs, ks, vs = [], [], []
        for b in range(nb):
            qb, kb_, vb_ = _conv_head_inputs(xbuf_ref, cw_ref, b * 16 + 8, steps, h)
            qs.append(qb), ks.append(kb_), vs.append(vb_)
        q, k, v = (jnp.concatenate(t, axis=0) for t in (qs, ks, vs))
        g_c = gcum[:, 8 + h:9 + h]
        g_r = gcum_t[8 + h:9 + h, :]
        g_e = gend[:, 8 + h:9 + h]
        u, w, qk, qg, kdec = _wy_block(q, k, v, beta[:, h:h + 1], g_c, g_r, g_e, incl, strict, 3)
        v_news, o_s = [], []
        for b in range(nb):
            r0, r1 = b * steps, (b + 1) * steps
            s = s0_ref[b, h]
            ws = _bdot(jnp.concatenate([w[r0:r1], qg[r0:r1]], axis=0), s)
            v_new = u[r0:r1] - ws[:steps]
            v_news.append(v_new)
            o_s.append(ws[steps:])
            s_out_ref[b, h] = s * jnp.exp(g_e[r1 - 1:r1, :]) + _bdot_tn(kdec[r0:r1], v_new)
        o = jnp.concatenate(o_s, axis=0) + _bdot(qk, jnp.concatenate(v_news, axis=0))
        z = proj_ref[:, pl.ds(CONV_DIM + h * A_DV, A_DV)]
        o_ref[:, pl.ds(h * A_DV, A_DV)] = _gate_out(o, z, nw_ref[...]).astype(o_ref.dtype)


def _gdn_sample(proj, obuf, conv0, s0, conv_w, alog_row, dtb_row, norm_w, row0, n_batch):
    blk0 = row0 // CHUNK
    vec = lambda n: pl.BlockSpec((1, n), lambda i: (0, 0))
    st = pl.BlockSpec((SAMPLE_BB, A_HEADS, A_DK, A_DV), lambda i: (i, 0, 0, 0))
    cs = pl.BlockSpec((SAMPLE_BB, CONV_W - 1, CONV_DIM), lambda i: (i, 0, 0))
    return pl.pallas_call(
        _gdn_sample_kernel,
        grid=(n_batch // SAMPLE_BB,),
        in_specs=[pl.BlockSpec((CHUNK, A_PROJ_PAD), lambda i: (blk0 + i, 0)),
                  cs, st,
                  pl.BlockSpec((CONV_W, CONV_DIM), lambda i: (0, 0)),
                  vec(128), vec(128), vec(A_DV),
                  pl.BlockSpec(memory_space=pl.ANY)],
        out_specs=[pl.BlockSpec((CHUNK, D_MODEL), lambda i: (blk0 + i, 0)), st, cs],
        out_shape=[jax.ShapeDtypeStruct(obuf.shape, obuf.dtype),
                   jax.ShapeDtypeStruct(s0.shape, F32),
                   jax.ShapeDtypeStruct(conv0.shape, F32)],
        scratch_shapes=[pltpu.VMEM((SAMPLE_BB * 16, CONV_DIM), F32)],
        input_output_aliases={7: 0},
        compiler_params=_cparams("arbitrary"),
        name="gdn_sample",
    )(proj, conv0, s0, conv_w, alog_row, dtb_row, norm_w.reshape(1, A_DV), obuf)


def _alibi_slope(h):
    return 2.0 ** (-8.0 * (h + 1) / B_HEADS)


def _sink_softmax_pv(pieces, sink):
    m = sink
    for s, _ in pieces:
        m = jnp.maximum(m, jnp.max(s, axis=-1, keepdims=True))
    den = jnp.exp(sink - m)
    acc = None
    ps = []
    for s, _ in pieces:
        p = jnp.exp(s - m)
        den = den + jnp.sum(p, axis=-1, keepdims=True)
        ps.append(p)
    inv = 1.0 / den
    for p, (_, v) in zip(ps, pieces):
        t = _bdot(p * inv, v)
        acc = t if acc is None else acc + t
    return acc


def _attn_prompt_kernel(sink_ref, q_ref, kvp_ref, kvc_ref, obuf_ref, o_ref):
    del obuf_ref
    i = pl.program_id(1)
    r = lax.broadcasted_iota(jnp.int32, (ROWS, 2 * ROWS), 0)
    c = lax.broadcasted_iota(jnp.int32, (ROWS, 2 * ROWS), 1)
    dist = r - c + ROWS
    kpos = i * ROWS - ROWS - LEAD + c
    valid = (kpos >= 0) & (dist >= 0) & (dist <= WINDOW)
    dist_f = dist.astype(F32)
    for g in range(B_KV_HEADS):
        k = jnp.concatenate([kvp_ref[:, pl.ds(g * B_HD, B_HD)], kvc_ref[:, pl.ds(g * B_HD, B_HD)]], axis=0)
        v = jnp.concatenate([kvp_ref[:, pl.ds(B_KV + g * B_HD, B_HD)], kvc_ref[:, pl.ds(B_KV + g * B_HD, B_HD)]], axis=0)
        kb, vb = k.astype(BF16), v.astype(BF16)
        for j in range(B_GROUP):
            h = g * B_GROUP + j
            s = _bdot_nt(q_ref[:, pl.ds(h * B_HD, B_HD)], kb) * (B_HD ** -0.5)
            s = jnp.where(valid, s - _alibi_slope(h) * dist_f, NEG_INF)
            o = _sink_softmax_pv([(s, vb)], sink_ref[h])
            o_ref[:, pl.ds(h * B_HD, B_HD)] = o.astype(o_ref.dtype)


def _attn_prompt(q, kv, obuf, sinks, n_batch, lp):
    nblk = lp // ROWS
    return pl.pallas_call(
        _attn_prompt_kernel,
        grid_spec=pltpu.PrefetchScalarGridSpec(
            num_scalar_prefetch=1,
            grid=(n_batch, nblk),
            in_specs=[pl.BlockSpec((ROWS, D_MODEL), lambda b, i, s: (b * nblk + i, 0)),
                      pl.BlockSpec((ROWS, 2 * B_KV), lambda b, i, s: (b * nblk + jnp.maximum(i - 1, 0), 0)),
                      pl.BlockSpec((ROWS, 2 * B_KV), lambda b, i, s: (b * nblk + i, 0)),
                      pl.BlockSpec(memory_space=pl.ANY)],
            out_specs=pl.BlockSpec((ROWS, D_MODEL), lambda b, i, s: (b * nblk + i, 0)),
        ),
        out_shape=jax.ShapeDtypeStruct(obuf.shape, obuf.dtype),
        input_output_aliases={4: 0},
        compiler_params=_cparams("arbitrary", "arbitrary"),
        name="attn_prompt",
    )(sinks, q, kv, kv, obuf)


ATT_BB = 8
ATT_S = 8


def _attn_sample_kernel(sink_ref, q_ref, kvn_ref, ck_ref, cv_ref, obuf_ref, o_ref):
    del obuf_ref
    bt = ATT_BB * ATT_S
    rows = B_GROUP * bt
    r = lax.broadcasted_iota(jnp.int32, (rows, ATT_BB * WINDOW), 0)
    c = lax.broadcasted_iota(jnp.int32, (rows, ATT_BB * WINDOW), 1)
    dist_c = WINDOW + r % ATT_S - c % WINDOW
    valid_c = ((r % bt) // ATT_S == c // WINDOW) & (dist_c <= WINDOW)
    dist_cf = dist_c.astype(F32)
    rn = lax.broadcasted_iota(jnp.int32, (rows, bt), 0)
    cn = lax.broadcasted_iota(jnp.int32, (rows, bt), 1)
    dist_n = rn % ATT_S - cn % ATT_S
    valid_n = ((rn % bt) // ATT_S == cn // ATT_S) & (dist_n >= 0)
    dist_nf = dist_n.astype(F32)
    hrow = lax.broadcasted_iota(jnp.int32, (rows, 1), 0) // bt
    for g in range(B_KV_HEADS):
        slope = jnp.zeros((rows, 1), F32)
        sink = jnp.zeros((rows, 1), F32)
        for j in range(B_GROUP):
            slope = jnp.where(hrow == j, _alibi_slope(g * B_GROUP + j), slope)
            sink = jnp.where(hrow == j, sink_ref[g * B_GROUP + j], sink)
        q = jnp.concatenate([q_ref[:, pl.ds((g * B_GROUP + j) * B_HD, B_HD)] for j in range(B_GROUP)], axis=0)
        kc = jnp.concatenate([ck_ref[b, :, pl.ds(g * B_HD, B_HD)] for b in range(ATT_BB)], axis=0)
        vc = jnp.concatenate([cv_ref[b, :, pl.ds(g * B_HD, B_HD)] for b in range(ATT_BB)], axis=0)
        kn = kvn_ref[:, pl.ds(g * B_HD, B_HD)]
        vn = kvn_ref[:, pl.ds(B_KV + g * B_HD, B_HD)]
        s_c = _bdot_nt(q, kc) * (B_HD ** -0.5)
        s_c = jnp.where(valid_c, s_c - slope * dist_cf, NEG_INF)
        s_n = _bdot_nt(q, kn) * (B_HD ** -0.5)
        s_n = jnp.where(valid_n, s_n - slope * dist_nf, NEG_INF)
        o = _sink_softmax_pv([(s_c, vc), (s_n, vn)], sink)
        for j in range(B_GROUP):
            h = g * B_GROUP + j
            o_ref[:, pl.ds(h * B_HD, B_HD)] = o[j * bt:(j + 1) * bt].astype(o_ref.dtype)


def _attn_sample(q, kv, obuf, cache_k, cache_v, sinks, row0, n_batch):
    rows = ATT_BB * ATT_S
    blk0 = row0 // rows
    cache = pl.BlockSpec((ATT_BB, WINDOW, B_KV), lambda i, s: (i, 0, 0))
    return pl.pallas_call(
        _attn_sample_kernel,
        grid_spec=pltpu.PrefetchScalarGridSpec(
            num_scalar_prefetch=1,
            grid=(n_batch // ATT_BB,),
            in_specs=[pl.BlockSpec((rows, D_MODEL), lambda i, s: (blk0 + i, 0)),
                      pl.BlockSpec((rows, 2 * B_KV), lambda i, s: (blk0 + i, 0)),
                      cache, cache,
                      pl.BlockSpec(memory_space=pl.ANY)],
            out_specs=pl.BlockSpec((rows, D_MODEL), lambda i, s: (blk0 + i, 0)),
        ),
        out_shape=jax.ShapeDtypeStruct(obuf.shape, obuf.dtype),
        input_output_aliases={5: 0},
        compiler_params=_cparams("arbitrary"),
        name="attn_sample",
    )(sinks, q, kv, cache_k, cache_v, obuf)


def _split2(x):
    hi = x.astype(BF16)
    lo = (x - hi.astype(F32)).astype(BF16)
    return hi, lo


def _router_kernel(x_ref, w_ref, b_ref, idx_ref, gate_ref, rank_ref, cnt_ref, run_ref):
    @pl.when(pl.program_id(0) == 0)
    def _():
        run_ref[...] = jnp.zeros(run_ref.shape, F32)

    xs = _split2(x_ref[...])
    ws = _split2(w_ref[...])
    logits = b_ref[...]
    for i, j in ((1, 1), (0, 1), (1, 0), (0, 0)):
        logits = logits + jnp.dot(xs[i], ws[j], preferred_element_type=F32)
    lane = lax.broadcasted_iota(jnp.int32, logits.shape, 1)
    real = lane < N_EXPERTS
    logits = jnp.where(real, logits, NEG_INF)
    m = jnp.max(logits, axis=-1, keepdims=True)
    e = jnp.where(real, jnp.exp(logits - m), 0.0)
    probs = e / jnp.sum(e, axis=-1, keepdims=True)
    p1 = jnp.max(probs, axis=-1, keepdims=True)
    i1 = jnp.min(jnp.where(probs == p1, lane, 128), axis=-1, keepdims=True)
    rest = jnp.where((lane == i1) | (~real), -1.0, probs)
    p2 = jnp.max(rest, axis=-1, keepdims=True)
    i2 = jnp.min(jnp.where(rest == p2, lane, 128), axis=-1, keepdims=True)
    tot = p1 + p2
    idx_ref[...] = jnp.where(lane == 0, i1, jnp.where(lane == 1, i2, 0))
    gate_ref[...] = jnp.where(lane == 0, p1 / tot, jnp.where(lane == 1, p2 / tot, 0.0))

    tm = logits.shape[0]
    chosen = ((lane == i1) | (lane == i2)).astype(F32)
    earlier = (lax.broadcasted_iota(jnp.int32, (tm, tm), 0) > lax.broadcasted_iota(jnp.int32, (tm, tm), 1))
    before = jnp.dot(earlier.astype(BF16), chosen.astype(BF16), preferred_element_type=F32) + run_ref[...]
    r1 = jnp.sum(jnp.where(lane == i1, before, 0.0), axis=-1, keepdims=True)
    r2 = jnp.sum(jnp.where(lane == i2, before, 0.0), axis=-1, keepdims=True)
    rank_ref[...] = jnp.where(lane == 0, r1, jnp.where(lane == 1, r2, 0.0)).astype(jnp.int32)
    run_ref[...] = run_ref[...] + jnp.sum(chosen, axis=0, keepdims=True)
    cnt_ref[...] = run_ref[...]


def _router(x, w_router, b_router, tm):
    m, d = x.shape
    w = jnp.zeros((d, 128), F32).at[:, :N_EXPERTS].set(w_router)
    b = jnp.zeros((1, 128), F32).at[0, :N_EXPERTS].set(b_router)
    out = pl.BlockSpec((tm, 128), lambda i: (i, 0))
    one = pl.BlockSpec((1, 128), lambda i: (0, 0))
    return pl.pallas_call(
        _router_kernel,
        grid=(m // tm,),
        in_specs=[pl.BlockSpec((tm, d), lambda i: (i, 0)), pl.BlockSpec((d, 128), lambda i: (0, 0)), one],
        out_specs=[out, out, out, one],
        out_shape=[jax.ShapeDtypeStruct((m, 128), jnp.int32), jax.ShapeDtypeStruct((m, 128), F32),
                   jax.ShapeDtypeStruct((m, 128), jnp.int32), jax.ShapeDtypeStruct((1, 128), F32)],
        scratch_shapes=[pltpu.VMEM((1, 128), F32)],
        compiler_params=_cparams("arbitrary"),
        name="moe_router",
    )(x, w, b)


def _moe_ffn_kernel(te_ref, xs_ref, wgu_ref, wd_ref, o_ref, wgub_ref, wdb_ref):
    i = pl.program_id(0)

    @pl.when((i == 0) | (te_ref[i] != te_ref[jnp.maximum(i - 1, 0)]))
    def _():
        wgub_ref[...] = wgu_ref[0].astype(BF16)
        wdb_ref[...] = wd_ref[0].astype(BF16)

    d_e = wd_ref.shape[1]
    gu = jnp.dot(xs_ref[...].astype(BF16), wgub_ref[...], preferred_element_type=F32)
    h = (_silu(gu[:, :d_e]) * gu[:, d_e:]).astype(BF16)
    o_ref[...] = jnp.dot(h, wdb_ref[...], preferred_element_type=F32)


def _moe_ffn(tile_expert, xs, w_gu, w_down):
    a_pad, d = xs.shape
    d_e = w_down.shape[1]
    return pl.pallas_call(
        _moe_ffn_kernel,
        grid_spec=pltpu.PrefetchScalarGridSpec(
            num_scalar_prefetch=1,
            grid=(a_pad // MOE_TM,),
            in_specs=[pl.BlockSpec((MOE_TM, d), lambda i, te: (i, 0)),
                      pl.BlockSpec((1, d, 2 * d_e), lambda i, te: (te[i], 0, 0)),
                      pl.BlockSpec((1, d_e, d), lambda i, te: (te[i], 0, 0))],
            out_specs=pl.BlockSpec((MOE_TM, d), lambda i, te: (i, 0)),
            scratch_shapes=[pltpu.VMEM((d, 2 * d_e), BF16), pltpu.VMEM((d_e, d), BF16)],
        ),
        out_shape=jax.ShapeDtypeStruct((a_pad, d), F32),
        compiler_params=_cparams("arbitrary"),
        name="moe_experts",
    )(tile_expert, xs, w_gu, w_down)


def _moe_layer(x, xb, w_router, b_router, w_gu, w_down, layer, ln_g, ln_b, tm):
    tt = x.shape[0]
    idx, gate, rank, cnt = _router(x, w_router, b_router, tm)
    counts = cnt[0, :N_EXPERTS].astype(jnp.int32)
    padded = ((counts + MOE_TM - 1) // MOE_TM) * MOE_TM
    pend = jnp.cumsum(padded)
    pstart = pend - padded
    dest_l = rank
    for e in range(N_EXPERTS):
        dest_l = dest_l + jnp.where(idx == e, pstart[e], 0)
    dest = dest_l[:, :2].reshape(-1)
    a_pad = 2 * tt + N_EXPERTS * MOE_TM
    src_tok = jnp.zeros((a_pad,), jnp.int32).at[dest].set(jnp.arange(2 * tt, dtype=jnp.int32) // 2)
    tile_start = jnp.arange(a_pad // MOE_TM, dtype=jnp.int32) * MOE_TM
    tile_expert = jnp.minimum(jnp.sum(tile_start[:, None] >= pend[None, :], axis=1), N_EXPERTS - 1).astype(jnp.int32)
    del xb
    xs = jnp.take(x, src_tok, axis=0)
    ys = _moe_ffn(tile_expert + layer * N_EXPERTS, xs, w_gu, w_down)
    dest2 = dest.reshape(tt, 2)
    y0 = jnp.take(ys, dest2[:, 0], axis=0)
    y1 = jnp.take(ys, dest2[:, 1], axis=0)
    return _add2_res_ln(y0, y1, gate, x, ln_g, ln_b, tm)


def kernel(x_prompt, x_sample, state_gdn, state_conv, cache_win_k, cache_win_v, meta_tokens, a_w_in, a_conv_w, a_A_log, a_dt_bias, a_norm_w, a_w_out, b_w_kv, b_w_q, b_sinks, b_w_o, ln_mix_g, ln_mix_b, ln_ffn_g, ln_ffn_b, ffd_w_gu, ffd_w_down, moe_w_router, moe_b_router, moe_w_gu, moe_w_down):
    nb, seq, d = x_prompt.shape
    nsb, steps, _ = x_sample.shape
    lp = LEAD + N_META + seq
    n_prompt = nb * lp
    tt = n_prompt + nsb * steps
    tm = 384 if tt % 384 == 0 else 128
    d_ff = ffd_w_down.shape[1]

    pieces = []
    for b in range(nb):
        pieces += [jnp.zeros((LEAD, d), F32), meta_tokens.astype(F32), x_prompt[b]]
    x = jnp.concatenate(pieces + [x_sample.reshape(nsb * steps, d)], axis=0)
    xb = x.astype(BF16)

    p_gdn, p_conv, s_gdn, s_conv = [], [], [], []
    kv = None
    n_moe, n_exp = moe_w_gu.shape[:2]
    moe_gu_all = moe_w_gu.reshape(n_moe * n_exp, *moe_w_gu.shape[2:])
    moe_down_all = moe_w_down.reshape(n_moe * n_exp, *moe_w_down.shape[2:])
    mix_in = jnp.zeros((tt, d), BF16)
    for l in range(DEPTH):
        if l < N_A_LAYERS:
            w_in = jnp.concatenate([a_w_in[l], jnp.zeros((d, A_PROJ_PAD - a_w_in.shape[2]), F32)], axis=1)
            proj = _matmul(xb, w_in, F32, tm, A_PROJ_PAD // 3)
            alog_row = jnp.zeros((1, 128), F32).at[0, 8:16].set(a_A_log[l])
            dtb_row = jnp.zeros((1, 128), F32).at[0, 8:16].set(a_dt_bias[l])
            mix_in, s_p, c_p = _gdn_prompt(proj, mix_in, a_conv_w[l], alog_row, dtb_row, a_norm_w[l], nb, lp)
            mix_in, s_s, c_s = _gdn_sample(proj, mix_in, state_conv[l], state_gdn[l], a_conv_w[l], alog_row, dtb_row,
                                           a_norm_w[l], n_prompt, nsb)
            p_gdn.append(s_p)
            s_gdn.append(s_s)
            p_conv.append(c_p[:, 8 - (CONV_W - 1):])
            s_conv.append(c_s)
            w_mix = a_w_out[l]
        else:
            j = l - N_A_LAYERS
            if j == 0:
                kv = _matmul(xb, b_w_kv, F32, tm, 2 * B_KV)
            q = _matmul(xb, b_w_q[j], BF16, tm, d)
            mix_in = _attn_prompt(q, kv, mix_in, b_sinks[j], nb, lp)
            mix_in = _attn_sample(q, kv, mix_in, cache_win_k.reshape(nsb, WINDOW, B_KV),
                                  cache_win_v.reshape(nsb, WINDOW, B_KV), b_sinks[j], n_prompt, nsb)
            w_mix = b_w_o[j]
        x, xb = _matmul_res_ln(mix_in, w_mix, x, ln_mix_g[l], ln_mix_b[l], tm)
        if l % 2 == 0:
            hmid = _matmul_swiglu(xb, ffd_w_gu[l // 2], d_ff, tm, d_ff // 2)
            x, xb = _matmul_res_ln(hmid, ffd_w_down[l // 2], x, ln_ffn_g[l], ln_ffn_b[l], tm)
        else:
            x, xb = _moe_layer(x, xb, moe_w_router[l // 2], moe_b_router[l // 2], moe_gu_all, moe_down_all, l // 2,
                               ln_ffn_g[l], ln_ffn_b[l], tm)

    y_prompt = x[:n_prompt].reshape(nb, lp, d)[:, LEAD + N_META:]
    y_sample = x[n_prompt:].reshape(nsb, steps, d)
    kvp = jnp.stack([kv[(b + 1) * lp - WINDOW:(b + 1) * lp] for b in range(nb)]).reshape(nb, WINDOW, 2, B_KV_HEADS, B_HD)
    kvs = kv[n_prompt:].reshape(nsb, steps, 2, B_KV_HEADS, B_HD)
    s_wk = jnp.concatenate([cache_win_k[:, steps:], kvs[:, :, 0]], axis=1)
    s_wv = jnp.concatenate([cache_win_v[:, steps:], kvs[:, :, 1]], axis=1)
    return (y_prompt, y_sample, jnp.stack(p_gdn), jnp.stack(p_conv), kvp[:, :, 0], kvp[:, :, 1],
            jnp.stack(s_gdn), jnp.stack(s_conv), s_wk, s_wv)
```

```python
import functools
import math

import jax
import jax.numpy as jnp
from jax import lax
from jax.experimental import pallas as pl
from jax.experimental.pallas import tpu as pltpu

F32 = jnp.float32
BF16 = jnp.bfloat16

D_MODEL = 1024
N_META = 16
A_HEADS = 8
A_DK = 128
A_DV = 128
CONV_W = 4
CONV_DIM = A_HEADS * (2 * A_DK + A_DV)
A_MAIN = CONV_DIM + A_HEADS * A_DV
A_PROJ_PAD = A_MAIN + 128
B_HD = 64
B_HEADS = 16
B_KV_HEADS = 4
B_GROUP = 4
B_KV = B_KV_HEADS * B_HD
WINDOW = 128
N_EXPERTS = 8
DEPTH = 4
N_A_LAYERS = 2
DEEPNORM_ALPHA = (2 * DEPTH) ** 0.25
LN_EPS = 1e-5
RMS_EPS = 1e-6
NEG_INF = -1e30

ROWS = 128
CHUNK = 64
INV_BASE = 32
LEAD = (-N_META) % ROWS
SAMPLE_BB = CHUNK // 8
MOE_TM = 256
VMEM_LIMIT = 56 * 1024 * 1024


def _cparams(*sem):
    return pltpu.CompilerParams(dimension_semantics=sem, vmem_limit_bytes=VMEM_LIMIT)


def _bdot(a, b):
    return jnp.dot(a.astype(BF16), b.astype(BF16), preferred_element_type=F32)


def _bdot_nt(a, b):
    return lax.dot_general(a.astype(BF16), b.astype(BF16), (((1,), (1,)), ((), ())), preferred_element_type=F32)


def _bdot_tn(a, b):
    return lax.dot_general(a.astype(BF16), b.astype(BF16), (((0,), (0,)), ((), ())), preferred_element_type=F32)


def _sigmoid(x):
    return 1.0 / (1.0 + jnp.exp(-x))


def _silu(x):
    return x * _sigmoid(x)


def _softplus(x):
    return jnp.maximum(x, 0.0) + jnp.log(1.0 + jnp.exp(-jnp.abs(x)))


def _layer_norm_rows(v, g, b):
    mu = jnp.mean(v, axis=-1, keepdims=True)
    vc = v - mu
    var = jnp.mean(vc * vc, axis=-1, keepdims=True)
    return vc * lax.rsqrt(var + LN_EPS) * g + b


def _mm_kernel(x_ref, w_ref, o_ref, wb_ref):
    @pl.when(pl.program_id(1) == 0)
    def _():
        wb_ref[...] = w_ref[...].astype(BF16)

    o_ref[...] = jnp.dot(x_ref[...], wb_ref[...], preferred_element_type=F32).astype(o_ref.dtype)


def _matmul(x, w, out_dtype, tm, tn):
    m, k = x.shape
    n = w.shape[1]
    return pl.pallas_call(
        _mm_kernel,
        grid=(n // tn, m // tm),
        in_specs=[pl.BlockSpec((tm, k), lambda j, i: (i, 0)), pl.BlockSpec((k, tn), lambda j, i: (0, j))],
        out_specs=pl.BlockSpec((tm, tn), lambda j, i: (i, j)),
        out_shape=jax.ShapeDtypeStruct((m, n), out_dtype),
        scratch_shapes=[pltpu.VMEM((k, tn), BF16)],
        compiler_params=_cparams("arbitrary", "arbitrary"),
        name="matmul",
    )(x, w)


def _swiglu_kernel(x_ref, wg_ref, wu_ref, o_ref, wgb_ref, wub_ref):
    @pl.when(pl.program_id(1) == 0)
    def _():
        wgb_ref[...] = wg_ref[...].astype(BF16)
        wub_ref[...] = wu_ref[...].astype(BF16)

    x = x_ref[...]
    g = jnp.dot(x, wgb_ref[...], preferred_element_type=F32)
    u = jnp.dot(x, wub_ref[...], preferred_element_type=F32)
    o_ref[...] = (_silu(g) * u).astype(o_ref.dtype)


def _matmul_swiglu(x, w_gu, d_ff, tm, tn):
    m, k = x.shape
    nt = d_ff // tn
    return pl.pallas_call(
        _swiglu_kernel,
        grid=(nt, m // tm),
        in_specs=[pl.BlockSpec((tm, k), lambda j, i: (i, 0)),
                  pl.BlockSpec((k, tn), lambda j, i: (0, j)),
                  pl.BlockSpec((k, tn), lambda j, i: (0, j + nt))],
        out_specs=pl.BlockSpec((tm, tn), lambda j, i: (i, j)),
        out_shape=jax.ShapeDtypeStruct((m, d_ff), BF16),
        scratch_shapes=[pltpu.VMEM((k, tn), BF16), pltpu.VMEM((k, tn), BF16)],
        compiler_params=_cparams("arbitrary", "arbitrary"),
        name="matmul_swiglu",
    )(x, w_gu, w_gu)


def _mm_res_ln_kernel(h_ref, w_ref, x_ref, g_ref, b_ref, o_ref, ob_ref, wb_ref):
    @pl.when(pl.program_id(0) == 0)
    def _():
        wb_ref[...] = w_ref[...].astype(BF16)

    f = jnp.dot(h_ref[...], wb_ref[...], preferred_element_type=F32)
    y = _layer_norm_rows(DEEPNORM_ALPHA * x_ref[...] + f, g_ref[...], b_ref[...])
    o_ref[...] = y
    ob_ref[...] = y.astype(BF16)


def _matmul_res_ln(h, w, x, g, b, tm):
    m, k = h.shape
    d = w.shape[1]
    return pl.pallas_call(
        _mm_res_ln_kernel,
        grid=(m // tm,),
        in_specs=[pl.BlockSpec((tm, k), lambda i: (i, 0)), pl.BlockSpec((k, d), lambda i: (0, 0)),
                  pl.BlockSpec((tm, d), lambda i: (i, 0)),
                  pl.BlockSpec((1, d), lambda i: (0, 0)), pl.BlockSpec((1, d), lambda i: (0, 0))],
        out_specs=[pl.BlockSpec((tm, d), lambda i: (i, 0)), pl.BlockSpec((tm, d), lambda i: (i, 0))],
        out_shape=[jax.ShapeDtypeStruct((m, d), F32), jax.ShapeDtypeStruct((m, d), BF16)],
        scratch_shapes=[pltpu.VMEM((k, d), BF16)],
        compiler_params=_cparams("arbitrary"),
        name="matmul_res_ln",
    )(h, w, x, g.reshape(1, d), b.reshape(1, d))


def _add2_res_ln_kernel(y0_ref, y1_ref, gate_ref, x_ref, g_ref, b_ref, o_ref, ob_ref):
    f = gate_ref[:, 0:1] * y0_ref[...] + gate_ref[:, 1:2] * y1_ref[...]
    y = _layer_norm_rows(DEEPNORM_ALPHA * x_ref[...] + f, g_ref[...], b_ref[...])
    o_ref[...] = y
    ob_ref[...] = y.astype(BF16)


def _add2_res_ln(y0, y1, gate, x, g, b, tm):
    m, d = x.shape
    row = pl.BlockSpec((tm, d), lambda i: (i, 0))
    vec = pl.BlockSpec((1, d), lambda i: (0, 0))
    return pl.pallas_call(
        _add2_res_ln_kernel,
        grid=(m // tm,),
        in_specs=[row, row, pl.BlockSpec((tm, 128), lambda i: (i, 0)), row, vec, vec],
        out_specs=[row, row],
        out_shape=[jax.ShapeDtypeStruct((m, d), F32), jax.ShapeDtypeStruct((m, d), BF16)],
        compiler_params=_cparams("arbitrary"),
        name="moe_combine_res_ln",
    )(y0, y1, gate, x, g.reshape(1, d), b.reshape(1, d))


def _gdn_gates(ba, alog_row, dtb_row, valid):
    beta = _sigmoid(ba)
    g = -jnp.exp(alog_row) * _softplus(ba + dtb_row)
    if valid is not None:
        beta = jnp.where(valid, beta, 0.0)
        g = jnp.where(valid, g, 0.0)
    return beta, g


def _group_cumsum(g, group):
    r_in = lax.broadcasted_iota(jnp.int32, g.shape, 0) % group
    s = 1
    while s < group:
        g = g + jnp.where(r_in >= s, pltpu.roll(g, s, axis=0), 0.0)
        s *= 2
    return g


def _group_last(g, group):
    rows = g.shape[0]
    r_in = lax.broadcasted_iota(jnp.int32, g.shape, 0) % group
    x = jnp.where(r_in == group - 1, g, 0.0)
    s = 1
    while s < group:
        x = x + jnp.where(r_in + s < group, pltpu.roll(x, rows - s, axis=0), 0.0)
        s *= 2
    return x


def _l2norm_rows(t):
    return t * lax.rsqrt(jnp.sum(t * t, axis=-1, keepdims=True) + 1e-6)


def _wy_block(q, k, v, beta_c, g_c, g_r, g_end_c, incl, strict, levels):
    c = q.shape[0]
    decay = jnp.where(incl, jnp.exp(jnp.where(incl, g_c - g_r, 0.0)), 0.0)
    kb = k * beta_c
    vb = v * beta_c
    a = jnp.where(strict, _bdot_nt(kb, k) * decay, 0.0)
    eye = (lax.broadcasted_iota(jnp.int32, (c, c), 0) == lax.broadcasted_iota(jnp.int32, (c, c), 1)).astype(F32)
    n = -a
    t = eye + n
    for _ in range(levels - 1):
        n = _bdot(n, n)
        t = t + _bdot(t, n)
    eg = jnp.exp(g_c)
    uw = _bdot(t, jnp.concatenate([vb, kb * eg], axis=1))
    u, w = uw[:, :A_DV], uw[:, A_DV:]
    qk = jnp.where(incl, _bdot_nt(q, k) * decay, 0.0)
    qg = q * eg
    kdec = k * jnp.exp(g_end_c - g_c)
    return u, w, qk, qg, kdec


def _gate_out(o, z, norm_w):
    o = o * lax.rsqrt(jnp.mean(o * o, axis=-1, keepdims=True) + RMS_EPS) * norm_w
    return o * _silu(z)


def _conv_head_inputs(xbuf_ref, cw_ref, row0, rows, h):
    outs = []
    for part in range(3):
        c0 = part * A_HEADS * A_DK + h * A_DK
        acc = xbuf_ref[pl.ds(row0 - 3, rows), pl.ds(c0, A_DK)] * cw_ref[0:1, pl.ds(c0, A_DK)]
        for j in range(1, CONV_W):
            acc = acc + xbuf_ref[pl.ds(row0 - 3 + j, rows), pl.ds(c0, A_DK)] * cw_ref[j:j + 1, pl.ds(c0, A_DK)]
        outs.append(_silu(acc))
    q = _l2norm_rows(outs[0]) * (A_DK ** -0.5)
    k = _l2norm_rows(outs[1])
    return q, k, outs[2]


def _blockdiag(a, b):
    z = jnp.zeros_like(a)
    return jnp.concatenate([jnp.concatenate([a, z], axis=1), jnp.concatenate([z, b], axis=1)], axis=0)


def _gdn_prompt_kernel(proj_ref, cw_ref, alog_ref, dtb_ref, nw_ref, obuf_ref, o_ref, s_out_ref, c_out_ref,
                       xbuf_ref, s_ref):
    del obuf_ref
    i = pl.program_id(1)

    @pl.when(i == 0)
    def _():
        xbuf_ref[0:8, :] = jnp.zeros((8, CONV_DIM), F32)
        s_ref[...] = jnp.zeros(s_ref.shape, F32)

    pos = i * ROWS + lax.broadcasted_iota(jnp.int32, (ROWS, 1), 0)
    valid = pos >= LEAD
    xbuf_ref[8:8 + ROWS, :] = jnp.where(valid, proj_ref[:, 0:CONV_DIM], 0.0)

    beta, g = _gdn_gates(proj_ref[:, A_MAIN:A_PROJ_PAD], alog_ref[...], dtb_ref[...], valid)
    gcum = _group_cumsum(g, ROWS)
    gcum_t = gcum.T

    ri = lax.broadcasted_iota(jnp.int32, (ROWS, ROWS), 0)
    ci = lax.broadcasted_iota(jnp.int32, (ROWS, ROWS), 1)
    incl = ri >= ci
    strict = ri > ci

    a_s, qk_s, rhs_s, qg_s, kdec_s, gend_s = [], [], [], [], [], []
    for h in range(A_HEADS):
        q, k, v = _conv_head_inputs(xbuf_ref, cw_ref, 8, ROWS, h)
        g_c = gcum[:, 8 + h:9 + h]
        g_r = gcum_t[8 + h:9 + h, :]
        g_end = gcum[ROWS - 1:ROWS, 8 + h:9 + h]
        decay = jnp.where(incl, jnp.exp(jnp.where(incl, g_c - g_r, 0.0)), 0.0)
        bcol = beta[:, h:h + 1]
        kb = k * bcol
        eg = jnp.exp(g_c)
        kq = _bdot_nt(jnp.concatenate([kb, q], axis=0), k)
        a_s.append(jnp.where(strict, kq[:ROWS] * decay, 0.0))
        qk_s.append(jnp.where(incl, kq[ROWS:] * decay, 0.0).astype(BF16))
        rhs_s.append(jnp.concatenate([v * bcol, kb * eg], axis=1).astype(BF16))
        qg_s.append((q * eg).astype(BF16))
        kdec_s.append((k * jnp.exp(g_end - g_c)).astype(BF16))
        gend_s.append(g_end)

    r2 = lax.broadcasted_iota(jnp.int32, (2 * ROWS, 2 * ROWS), 0)
    c2 = lax.broadcasted_iota(jnp.int32, (2 * ROWS, 2 * ROWS), 1)
    eye = (r2 == c2).astype(F32)
    pairs = A_HEADS // 2
    a_bd = [_blockdiag(a_s[2 * p], a_s[2 * p + 1]) for p in range(pairs)]
    base = (r2 // INV_BASE) == (c2 // INV_BASE)
    nbf = [jnp.where(base, -a, 0.0).astype(BF16) for a in a_bd]
    t32 = [eye + n.astype(F32) for n in nbf]
    size = 2
    while size < INV_BASE:
        nbf = [jnp.dot(n, n, preferred_element_type=F32).astype(BF16) for n in nbf]
        t32 = [t + jnp.dot(t.astype(BF16), n, preferred_element_type=F32) for t, n in zip(t32, nbf)]
        size *= 2
    while size < ROWS:
        off = ((r2 // (2 * size)) == (c2 // (2 * size))) & ((r2 // size) != (c2 // size))
        tbf = [t.astype(BF16) for t in t32]
        xs = [jnp.dot(jnp.where(off, a, 0.0).astype(BF16), t, preferred_element_type=F32).astype(BF16)
              for a, t in zip(a_bd, tbf)]
        t32 = [t - jnp.dot(tb, x, preferred_element_type=F32) for t, tb, x in zip(t32, tbf, xs)]
        size *= 2
    uw_s = [jnp.dot(t32[p].astype(BF16), jnp.concatenate([rhs_s[2 * p], rhs_s[2 * p + 1]], axis=0),
                    preferred_element_type=F32) for p in range(pairs)]

    for h in range(A_HEADS):
        uw = uw_s[h // 2][(h % 2) * ROWS:(h % 2 + 1) * ROWS]
        u, w = uw[:, :A_DV], uw[:, A_DV:]
        s = s_ref[h]
        ws = jnp.dot(jnp.concatenate([w.astype(BF16), qg_s[h]], axis=0), s.astype(BF16), preferred_element_type=F32)
        v_new = (u - ws[:ROWS]).astype(BF16)
        o = ws[ROWS:] + jnp.dot(qk_s[h], v_new, preferred_element_type=F32)
        s_ref[h] = s * jnp.exp(gend_s[h]) + lax.dot_general(kdec_s[h], v_new, (((0,), (0,)), ((), ())),
                                                            preferred_element_type=F32)
        z = proj_ref[:, pl.ds(CONV_DIM + h * A_DV, A_DV)]
        o_ref[:, pl.ds(h * A_DV, A_DV)] = _gate_out(o, z, nw_ref[...]).astype(o_ref.dtype)

    xbuf_ref[0:8, :] = xbuf_ref[ROWS:ROWS + 8, :]

    @pl.when(i == pl.num_programs(1) - 1)
    def _():
        s_out_ref[0] = s_ref[...]
        c_out_ref[0] = xbuf_ref[ROWS:ROWS + 8, :]


def _gdn_prompt(proj, obuf, conv_w, alog_row, dtb_row, norm_w, n_batch, lp):
    nblk = lp // ROWS
    vec = lambda n: pl.BlockSpec((1, n), lambda b, i: (0, 0))
    return pl.pallas_call(
        _gdn_prompt_kernel,
        grid=(n_batch, nblk),
        in_specs=[pl.BlockSpec((ROWS, A_PROJ_PAD), lambda b, i: (b * nblk + i, 0)),
                  pl.BlockSpec((CONV_W, CONV_DIM), lambda b, i: (0, 0)),
                  vec(128), vec(128), vec(A_DV),
                  pl.BlockSpec(memory_space=pl.ANY)],
        out_specs=[pl.BlockSpec((ROWS, D_MODEL), lambda b, i: (b * nblk + i, 0)),
                   pl.BlockSpec((1, A_HEADS, A_DK, A_DV), lambda b, i: (b, 0, 0, 0)),
                   pl.BlockSpec((1, 8, CONV_DIM), lambda b, i: (b, 0, 0))],
        out_shape=[jax.ShapeDtypeStruct(obuf.shape, obuf.dtype),
                   jax.ShapeDtypeStruct((n_batch, A_HEADS, A_DK, A_DV), F32),
                   jax.ShapeDtypeStruct((n_batch, 8, CONV_DIM), F32)],
        scratch_shapes=[pltpu.VMEM((ROWS + 8, CONV_DIM), F32), pltpu.VMEM((A_HEADS, A_DK, A_DV), F32)],
        input_output_aliases={5: 0},
        compiler_params=_cparams("arbitrary", "arbitrary"),
        name="gdn_prompt",
    )(proj, conv_w, alog_row, dtb_row, norm_w.reshape(1, A_DV), obuf)


def _gdn_sample_kernel(proj_ref, c0_ref, s0_ref, cw_ref, alog_ref, dtb_ref, nw_ref, obuf_ref, o_ref, s_out_ref,
                       c_out_ref, xbuf_ref):
    del obuf_ref
    nb, steps = SAMPLE_BB, CHUNK // SAMPLE_BB
    for b in range(nb):
        xbuf_ref[pl.ds(b * 16 + 5, 3), :] = c0_ref[b]
        xbuf_ref[pl.ds(b * 16 + 8, steps), :] = proj_ref[pl.ds(b * steps, steps), 0:CONV_DIM]
        c_out_ref[b] = proj_ref[pl.ds((b + 1) * steps - (CONV_W - 1), CONV_W - 1), 0:CONV_DIM]

    beta, g = _gdn_gates(proj_ref[:, A_MAIN:A_PROJ_PAD], alog_ref[...], dtb_ref[...], None)
    gcum = _group_cumsum(g, steps)
    gcum_t = gcum.T
    gend = _group_last(gcum, steps)

    ri = lax.broadcasted_iota(jnp.int32, (CHUNK, CHUNK), 0)
    ci = lax.broadcasted_iota(jnp.int32, (CHUNK, CHUNK), 1)
    same = (ri // steps) == (ci // steps)
    incl = same & (ri >= ci)
    strict = same & (ri > ci)

    for h in range(A_HEADS):
        qs, ks, vs = [], [], []
        for b in range(nb):
            qb, kb_, vb_ = _conv_head_inputs(xbuf_ref, cw_ref, b * 16 + 8, steps, h)
            qs.append(qb), ks.append(kb_), vs.append(vb_)
        q, k, v = (jnp.concatenate(t, axis=0) for t in (qs, ks, vs))
        g_c = gcum[:, 8 + h:9 + h]
        g_r = gcum_t[8 + h:9 + h, :]
        g_e = gend[:, 8 + h:9 + h]
        u, w, qk, qg, kdec = _wy_block(q, k, v, beta[:, h:h + 1], g_c, g_r, g_e, incl, strict, 3)
        v_news, o_s = [], []
        for b in range(nb):
            r0, r1 = b * steps, (b + 1) * steps
            s = s0_ref[b, h]
            ws = _bdot(jnp.concatenate([w[r0:r1], qg[r0:r1]], axis=0), s)
            v_new = u[r0:r1] - ws[:steps]
            v_news.append(v_new)
            o_s.append(ws[steps:])
            s_out_ref[b, h] = s * jnp.exp(g_e[r1 - 1:r1, :]) + _bdot_tn(kdec[r0:r1], v_new)
        o = jnp.concatenate(o_s, axis=0) + _bdot(qk, jnp.concatenate(v_news, axis=0))
        z = proj_ref[:, pl.ds(CONV_DIM + h * A_DV, A_DV)]
        o_ref[:, pl.ds(h * A_DV, A_DV)] = _gate_out(o, z, nw_ref[...]).astype(o_ref.dtype)


def _gdn_sample(proj, obuf, conv0, s0, conv_w, alog_row, dtb_row, norm_w, row0, n_batch):
    blk0 = row0 // CHUNK
    vec = lambda n: pl.BlockSpec((1, n), lambda i: (0, 0))
    st = pl.BlockSpec((SAMPLE_BB, A_HEADS, A_DK, A_DV), lambda i: (i, 0, 0, 0))
    cs = pl.BlockSpec((SAMPLE_BB, CONV_W - 1, CONV_DIM), lambda i: (i, 0, 0))
    return pl.pallas_call(
        _gdn_sample_kernel,
        grid=(n_batch // SAMPLE_BB,),
        in_specs=[pl.BlockSpec((CHUNK, A_PROJ_PAD), lambda i: (blk0 + i, 0)),
                  cs, st,
                  pl.BlockSpec((CONV_W, CONV_DIM), lambda i: (0, 0)),
                  vec(128), vec(128), vec(A_DV),
                  pl.BlockSpec(memory_space=pl.ANY)],
        out_specs=[pl.BlockSpec((CHUNK, D_MODEL), lambda i: (blk0 + i, 0)), st, cs],
        out_shape=[jax.ShapeDtypeStruct(obuf.shape, obuf.dtype),
                   jax.ShapeDtypeStruct(s0.shape, F32),
                   jax.ShapeDtypeStruct(conv0.shape, F32)],
        scratch_shapes=[pltpu.VMEM((SAMPLE_BB * 16, CONV_DIM), F32)],
        input_output_aliases={7: 0},
        compiler_params=_cparams("arbitrary"),
        name="gdn_sample",
    )(proj, conv0, s0, conv_w, alog_row, dtb_row, norm_w.reshape(1, A_DV), obuf)


def _alibi_slope(h):
    return 2.0 ** (-8.0 * (h + 1) / B_HEADS)


def _sink_softmax_pv(pieces, sink):
    m = sink
    for s, _ in pieces:
        m = jnp.maximum(m, jnp.max(s, axis=-1, keepdims=True))
    den = jnp.exp(sink - m)
    acc = None
    ps = []
    for s, _ in pieces:
        p = jnp.exp(s - m)
        den = den + jnp.sum(p, axis=-1, keepdims=True)
        ps.append(p)
    inv = 1.0 / den
    for p, (_, v) in zip(ps, pieces):
        t = _bdot(p * inv, v)
        acc = t if acc is None else acc + t
    return acc


def _attn_prompt_kernel(sink_ref, q_ref, kvp_ref, kvc_ref, obuf_ref, o_ref):
    del obuf_ref
    i = pl.program_id(1)
    r = lax.broadcasted_iota(jnp.int32, (ROWS, 2 * ROWS), 0)
    c = lax.broadcasted_iota(jnp.int32, (ROWS, 2 * ROWS), 1)
    dist = r - c + ROWS
    kpos = i * ROWS - ROWS - LEAD + c
    valid = (kpos >= 0) & (dist >= 0) & (dist <= WINDOW)
    dist_f = dist.astype(F32)
    for g in range(B_KV_HEADS):
        k = jnp.concatenate([kvp_ref[:, pl.ds(g * B_HD, B_HD)], kvc_ref[:, pl.ds(g * B_HD, B_HD)]], axis=0)
        v = jnp.concatenate([kvp_ref[:, pl.ds(B_KV + g * B_HD, B_HD)], kvc_ref[:, pl.ds(B_KV + g * B_HD, B_HD)]], axis=0)
        kb, vb = k.astype(BF16), v.astype(BF16)
        for j in range(B_GROUP):
            h = g * B_GROUP + j
            s = _bdot_nt(q_ref[:, pl.ds(h * B_HD, B_HD)], kb) * (B_HD ** -0.5)
            s = jnp.where(valid, s - _alibi_slope(h) * dist_f, NEG_INF)
            o = _sink_softmax_pv([(s, vb)], sink_ref[h])
            o_ref[:, pl.ds(h * B_HD, B_HD)] = o.astype(o_ref.dtype)


def _attn_prompt(q, kv, obuf, sinks, n_batch, lp):
    nblk = lp // ROWS
    return pl.pallas_call(
        _attn_prompt_kernel,
        grid_spec=pltpu.PrefetchScalarGridSpec(
            num_scalar_prefetch=1,
            grid=(n_batch, nblk),
            in_specs=[pl.BlockSpec((ROWS, D_MODEL), lambda b, i, s: (b * nblk + i, 0)),
                      pl.BlockSpec((ROWS, 2 * B_KV), lambda b, i, s: (b * nblk + jnp.maximum(i - 1, 0), 0)),
                      pl.BlockSpec((ROWS, 2 * B_KV), lambda b, i, s: (b * nblk + i, 0)),
                      pl.BlockSpec(memory_space=pl.ANY)],
            out_specs=pl.BlockSpec((ROWS, D_MODEL), lambda b, i, s: (b * nblk + i, 0)),
        ),
        out_shape=jax.ShapeDtypeStruct(obuf.shape, obuf.dtype),
        input_output_aliases={4: 0},
        compiler_params=_cparams("arbitrary", "arbitrary"),
        name="attn_prompt",
    )(sinks, q, kv, kv, obuf)


ATT_BB = 8
ATT_S = 8


def _attn_sample_kernel(sink_ref, q_ref, kvn_ref, ck_ref, cv_ref, obuf_ref, o_ref):
    del obuf_ref
    bt = ATT_BB * ATT_S
    rows = B_GROUP * bt
    r = lax.broadcasted_iota(jnp.int32, (rows, ATT_BB * WINDOW), 0)
    c = lax.broadcasted_iota(jnp.int32, (rows, ATT_BB * WINDOW), 1)
    dist_c = WINDOW + r % ATT_S - c % WINDOW
    valid_c = ((r % bt) // ATT_S == c // WINDOW) & (dist_c <= WINDOW)
    dist_cf = dist_c.astype(F32)
    rn = lax.broadcasted_iota(jnp.int32, (rows, bt), 0)
    cn = lax.broadcasted_iota(jnp.int32, (rows, bt), 1)
    dist_n = rn % ATT_S - cn % ATT_S
    valid_n = ((rn % bt) // ATT_S == cn // ATT_S) & (dist_n >= 0)
    dist_nf = dist_n.astype(F32)
    hrow = lax.broadcasted_iota(jnp.int32, (rows, 1), 0) // bt
    for g in range(B_KV_HEADS):
        slope = jnp.zeros((rows, 1), F32)
        sink = jnp.zeros((rows, 1), F32)
        for j in range(B_GROUP):
            slope = jnp.where(hrow == j, _alibi_slope(g * B_GROUP + j), slope)
            sink = jnp.where(hrow == j, sink_ref[g * B_GROUP + j], sink)
        q = jnp.concatenate([q_ref[:, pl.ds((g * B_GROUP + j) * B_HD, B_HD)] for j in range(B_GROUP)], axis=0)
        kc = jnp.concatenate([ck_ref[b, :, pl.ds(g * B_HD, B_HD)] for b in range(ATT_BB)], axis=0)
        vc = jnp.concatenate([cv_ref[b, :, pl.ds(g * B_HD, B_HD)] for b in range(ATT_BB)], axis=0)
        kn = kvn_ref[:, pl.ds(g * B_HD, B_HD)]
        vn = kvn_ref[:, pl.ds(B_KV + g * B_HD, B_HD)]
        s_c = _bdot_nt(q, kc) * (B_HD ** -0.5)
        s_c = jnp.where(valid_c, s_c - slope * dist_cf, NEG_INF)
        s_n = _bdot_nt(q, kn) * (B_HD ** -0.5)
        s_n = jnp.where(valid_n, s_n - slope * dist_nf, NEG_INF)
        o = _sink_softmax_pv([(s_c, vc), (s_n, vn)], sink)
        for j in range(B_GROUP):
            h = g * B_GROUP + j
            o_ref[:, pl.ds(h * B_HD, B_HD)] = o[j * bt:(j + 1) * bt].astype(o_ref.dtype)


def _attn_sample(q, kv, obuf, cache_k, cache_v, sinks, row0, n_batch):
    rows = ATT_BB * ATT_S
    blk0 = row0 // rows
    cache = pl.BlockSpec((ATT_BB, WINDOW, B_KV), lambda i, s: (i, 0, 0))
    return pl.pallas_call(
        _attn_sample_kernel,
        grid_spec=pltpu.PrefetchScalarGridSpec(
            num_scalar_prefetch=1,
            grid=(n_batch // ATT_BB,),
            in_specs=[pl.BlockSpec((rows, D_MODEL), lambda i, s: (blk0 + i, 0)),
                      pl.BlockSpec((rows, 2 * B_KV), lambda i, s: (blk0 + i, 0)),
                      cache, cache,
                      pl.BlockSpec(memory_space=pl.ANY)],
            out_specs=pl.BlockSpec((rows, D_MODEL), lambda i, s: (blk0 + i, 0)),
        ),
        out_shape=jax.ShapeDtypeStruct(obuf.shape, obuf.dtype),
        input_output_aliases={5: 0},
        compiler_params=_cparams("arbitrary"),
        name="attn_sample",
    )(sinks, q, kv, cache_k, cache_v, obuf)


def _split2(x):
    hi = x.astype(BF16)
    lo = (x - hi.astype(F32)).astype(BF16)
    return hi, lo


def _router_kernel(x_ref, w_ref, b_ref, idx_ref, gate_ref, rank_ref, cnt_ref, run_ref):
    @pl.when(pl.program_id(0) == 0)
    def _():
        run_ref[...] = jnp.zeros(run_ref.shape, F32)

    xs = _split2(x_ref[...])
    ws = _split2(w_ref[...])
    logits = b_ref[...]
    for i, j in ((1, 1), (0, 1), (1, 0), (0, 0)):
        logits = logits + jnp.dot(xs[i], ws[j], preferred_element_type=F32)
    lane = lax.broadcasted_iota(jnp.int32, logits.shape, 1)
    real = lane < N_EXPERTS
    logits = jnp.where(real, logits, NEG_INF)
    m = jnp.max(logits, axis=-1, keepdims=True)
    e = jnp.where(real, jnp.exp(logits - m), 0.0)
    probs = e / jnp.sum(e, axis=-1, keepdims=True)
    p1 = jnp.max(probs, axis=-1, keepdims=True)
    i1 = jnp.min(jnp.where(probs == p1, lane, 128), axis=-1, keepdims=True)
    rest = jnp.where((lane == i1) | (~real), -1.0, probs)
    p2 = jnp.max(rest, axis=-1, keepdims=True)
    i2 = jnp.min(jnp.where(rest == p2, lane, 128), axis=-1, keepdims=True)
    tot = p1 + p2
    idx_ref[...] = jnp.where(lane == 0, i1, jnp.where(lane == 1, i2, 0))
    gate_ref[...] = jnp.where(lane == 0, p1 / tot, jnp.where(lane == 1, p2 / tot, 0.0))

    tm = logits.shape[0]
    chosen = ((lane == i1) | (lane == i2)).astype(F32)
    earlier = (lax.broadcasted_iota(jnp.int32, (tm, tm), 0) > lax.broadcasted_iota(jnp.int32, (tm, tm), 1))
    before = jnp.dot(earlier.astype(BF16), chosen.astype(BF16), preferred_element_type=F32) + run_ref[...]
    r1 = jnp.sum(jnp.where(lane == i1, before, 0.0), axis=-1, keepdims=True)
    r2 = jnp.sum(jnp.where(lane == i2, before, 0.0), axis=-1, keepdims=True)
    rank_ref[...] = jnp.where(lane == 0, r1, jnp.where(lane == 1, r2, 0.0)).astype(jnp.int32)
    run_ref[...] = run_ref[...] + jnp.sum(chosen, axis=0, keepdims=True)
    cnt_ref[...] = run_ref[...]


def _router(x, w_router, b_router, tm):
    m, d = x.shape
    w = jnp.zeros((d, 128), F32).at[:, :N_EXPERTS].set(w_router)
    b = jnp.zeros((1, 128), F32).at[0, :N_EXPERTS].set(b_router)
    out = pl.BlockSpec((tm, 128), lambda i: (i, 0))
    one = pl.BlockSpec((1, 128), lambda i: (0, 0))
    return pl.pallas_call(
        _router_kernel,
        grid=(m // tm,),
        in_specs=[pl.BlockSpec((tm, d), lambda i: (i, 0)), pl.BlockSpec((d, 128), lambda i: (0, 0)), one],
        out_specs=[out, out, out, one],
        out_shape=[jax.ShapeDtypeStruct((m, 128), jnp.int32), jax.ShapeDtypeStruct((m, 128), F32),
                   jax.ShapeDtypeStruct((m, 128), jnp.int32), jax.ShapeDtypeStruct((1, 128), F32)],
        scratch_shapes=[pltpu.VMEM((1, 128), F32)],
        compiler_params=_cparams("arbitrary"),
        name="moe_router",
    )(x, w, b)


def _moe_ffn_kernel(te_ref, xs_ref, wgu_ref, wd_ref, o_ref, wgub_ref, wdb_ref):
    i = pl.program_id(0)

    @pl.when((i == 0) | (te_ref[i] != te_ref[jnp.maximum(i - 1, 0)]))
    def _():
        wgub_ref[...] = wgu_ref[0].astype(BF16)
        wdb_ref[...] = wd_ref[0].astype(BF16)

    d_e = wd_ref.shape[1]
    gu = jnp.dot(xs_ref[...].astype(BF16), wgub_ref[...], preferred_element_type=F32)
    h = (_silu(gu[:, :d_e]) * gu[:, d_e:]).astype(BF16)
    o_ref[...] = jnp.dot(h, wdb_ref[...], preferred_element_type=F32)


def _moe_ffn(tile_expert, xs, w_gu, w_down):
    a_pad, d = xs.shape
    d_e = w_down.shape[1]
    return pl.pallas_call(
        _moe_ffn_kernel,
        grid_spec=pltpu.PrefetchScalarGridSpec(
            num_scalar_prefetch=1,
            grid=(a_pad // MOE_TM,),
            in_specs=[pl.BlockSpec((MOE_TM, d), lambda i, te: (i, 0)),
                      pl.BlockSpec((1, d, 2 * d_e), lambda i, te: (te[i], 0, 0)),
                      pl.BlockSpec((1, d_e, d), lambda i, te: (te[i], 0, 0))],
            out_specs=pl.BlockSpec((MOE_TM, d), lambda i, te: (i, 0)),
            scratch_shapes=[pltpu.VMEM((d, 2 * d_e), BF16), pltpu.VMEM((d_e, d), BF16)],
        ),
        out_shape=jax.ShapeDtypeStruct((a_pad, d), F32),
        compiler_params=_cparams("arbitrary"),
        name="moe_experts",
    )(tile_expert, xs, w_gu, w_down)


def _moe_layer(x, xb, w_router, b_router, w_gu, w_down, layer, ln_g, ln_b, tm):
    tt = x.shape[0]
    idx, gate, rank, cnt = _router(x, w_router, b_router, tm)
    counts = cnt[0, :N_EXPERTS].astype(jnp.int32)
    padded = ((counts + MOE_TM - 1) // MOE_TM) * MOE_TM
    pend = jnp.cumsum(padded)
    pstart = pend - padded
    dest_l = rank
    for e in range(N_EXPERTS):
        dest_l = dest_l + jnp.where(idx == e, pstart[e], 0)
    dest = dest_l[:, :2].reshape(-1)
    a_pad = 2 * tt + N_EXPERTS * MOE_TM
    src_tok = jnp.zeros((a_pad,), jnp.int32).at[dest].set(jnp.arange(2 * tt, dtype=jnp.int32) // 2)
    tile_start = jnp.arange(a_pad // MOE_TM, dtype=jnp.int32) * MOE_TM
    tile_expert = jnp.minimum(jnp.sum(tile_start[:, None] >= pend[None, :], axis=1), N_EXPERTS - 1).astype(jnp.int32)
    del xb
    xs = jnp.take(x, src_tok, axis=0, mode="clip")
    ys = _moe_ffn(tile_expert + layer * N_EXPERTS, xs, w_gu, w_down)
    dest2 = dest.reshape(tt, 2)
    y0 = jnp.take(ys, dest2[:, 0], axis=0, mode="clip")
    y1 = jnp.take(ys, dest2[:, 1], axis=0, mode="clip")
    return _add2_res_ln(y0, y1, gate, x, ln_g, ln_b, tm)


def kernel(x_prompt, x_sample, state_gdn, state_conv, cache_win_k, cache_win_v, meta_tokens, a_w_in, a_conv_w, a_A_log, a_dt_bias, a_norm_w, a_w_out, b_w_kv, b_w_q, b_sinks, b_w_o, ln_mix_g, ln_mix_b, ln_ffn_g, ln_ffn_b, ffd_w_gu, ffd_w_down, moe_w_router, moe_b_router, moe_w_gu, moe_w_down):
    nb, seq, d = x_prompt.shape
    nsb, steps, _ = x_sample.shape
    lp = LEAD + N_META + seq
    n_prompt = nb * lp
    tt = n_prompt + nsb * steps
    tm = 384 if tt % 384 == 0 else 128
    d_ff = ffd_w_down.shape[1]

    pieces = []
    for b in range(nb):
        pieces += [jnp.zeros((LEAD, d), F32), meta_tokens.astype(F32), x_prompt[b]]
    x = jnp.concatenate(pieces + [x_sample.reshape(nsb * steps, d)], axis=0)
    xb = x.astype(BF16)

    p_gdn, p_conv, s_gdn, s_conv = [], [], [], []
    kv = None
    n_moe, n_exp = moe_w_gu.shape[:2]
    moe_gu_all = moe_w_gu.reshape(n_moe * n_exp, *moe_w_gu.shape[2:])
    moe_down_all = moe_w_down.reshape(n_moe * n_exp, *moe_w_down.shape[2:])
    mix_in = jnp.zeros((tt, d), BF16)
    for l in range(DEPTH):
        if l < N_A_LAYERS:
            w_in = jnp.concatenate([a_w_in[l], jnp.zeros((d, A_PROJ_PAD - a_w_in.shape[2]), F32)], axis=1)
            proj = _matmul(xb, w_in, F32, tm, A_PROJ_PAD // 3)
            alog_row = jnp.zeros((1, 128), F32).at[0, 8:16].set(a_A_log[l])
            dtb_row = jnp.zeros((1, 128), F32).at[0, 8:16].set(a_dt_bias[l])
            mix_in, s_p, c_p = _gdn_prompt(proj, mix_in, a_conv_w[l], alog_row, dtb_row, a_norm_w[l], nb, lp)
            mix_in, s_s, c_s = _gdn_sample(proj, mix_in, state_conv[l], state_gdn[l], a_conv_w[l], alog_row, dtb_row,
                                           a_norm_w[l], n_prompt, nsb)
            p_gdn.append(s_p)
            s_gdn.append(s_s)
            p_conv.append(c_p[:, 8 - (CONV_W - 1):])
            s_conv.append(c_s)
            w_mix = a_w_out[l]
        else:
            j = l - N_A_LAYERS
            if j == 0:
                kv = _matmul(xb, b_w_kv, F32, tm, 2 * B_KV)
            q = _matmul(xb, b_w_q[j], BF16, tm, d)
            mix_in = _attn_prompt(q, kv, mix_in, b_sinks[j], nb, lp)
            mix_in = _attn_sample(q, kv, mix_in, cache_win_k.reshape(nsb, WINDOW, B_KV),
                                  cache_win_v.reshape(nsb, WINDOW, B_KV), b_sinks[j], n_prompt, nsb)
            w_mix = b_w_o[j]
        x, xb = _matmul_res_ln(mix_in, w_mix, x, ln_mix_g[l], ln_mix_b[l], tm)
        if l % 2 == 0:
            hmid = _matmul_swiglu(xb, ffd_w_gu[l // 2], d_ff, tm, d_ff // 2)
            x, xb = _matmul_res_ln(hmid, ffd_w_down[l // 2], x, ln_ffn_g[l], ln_ffn_b[l], tm)
        else:
            x, xb = _moe_layer(x, xb, moe_w_router[l // 2], moe_b_router[l // 2], moe_gu_all, moe_down_all, l // 2,
                               ln_ffn_g[l], ln_ffn_b[l], tm)

    y_prompt = x[:n_prompt].reshape(nb, lp, d)[:, LEAD + N_META:]
    y_sample = x[n_prompt:].reshape(nsb, steps, d)
    kvp = jnp.stack([kv[(b + 1) * lp - WINDOW:(b + 1) * lp] for b in range(nb)]).reshape(nb, WINDOW, 2, B_KV_HEADS, B_HD)
    kvs = kv[n_prompt:].reshape(nsb, steps, 2, B_KV_HEADS, B_HD)
    s_wk = jnp.concatenate([cache_win_k[:, steps:], kvs[:, :, 0]], axis=1)
    s_wv = jnp.concatenate([cache_win_v[:, steps:], kvs[:, :, 1]], axis=1)
    return (y_prompt, y_sample, jnp.stack(p_gdn), jnp.stack(p_conv), kvp[:, :, 0], kvp[:, :, 1],
            jnp.stack(s_gdn), jnp.stack(s_conv), s_wk, s_wv)
```

```python
import functools
import math

import jax
import jax.numpy as jnp
from jax import lax
from jax.experimental import pallas as pl
from jax.experimental.pallas import tpu as pltpu

F32 = jnp.float32
BF16 = jnp.bfloat16

D_MODEL = 1024
N_META = 16
A_HEADS = 8
A_DK = 128
A_DV = 128
CONV_W = 4
CONV_DIM = A_HEADS * (2 * A_DK + A_DV)
A_MAIN = CONV_DIM + A_HEADS * A_DV
A_PROJ_PAD = A_MAIN + 128
B_HD = 64
B_HEADS = 16
B_KV_HEADS = 4
B_GROUP = 4
B_KV = B_KV_HEADS * B_HD
WINDOW = 128
N_EXPERTS = 8
DEPTH = 4
N_A_LAYERS = 2
DEEPNORM_ALPHA = (2 * DEPTH) ** 0.25
LN_EPS = 1e-5
RMS_EPS = 1e-6
NEG_INF = -1e30

ROWS = 128
CHUNK = 64
INV_BASE = 32
LEAD = (-N_META) % ROWS
SAMPLE_BB = CHUNK // 8
MOE_TM = 512
VMEM_LIMIT = 56 * 1024 * 1024


def _cparams(*sem):
    return pltpu.CompilerParams(dimension_semantics=sem, vmem_limit_bytes=VMEM_LIMIT)


def _bdot(a, b):
    return jnp.dot(a.astype(BF16), b.astype(BF16), preferred_element_type=F32)


def _bdot_nt(a, b):
    return lax.dot_general(a.astype(BF16), b.astype(BF16), (((1,), (1,)), ((), ())), preferred_element_type=F32)


def _bdot_tn(a, b):
    return lax.dot_general(a.astype(BF16), b.astype(BF16), (((0,), (0,)), ((), ())), preferred_element_type=F32)


def _sigmoid(x):
    return 1.0 / (1.0 + jnp.exp(-x))


def _silu(x):
    return x * _sigmoid(x)


def _softplus(x):
    return jnp.maximum(x, 0.0) + jnp.log(1.0 + jnp.exp(-jnp.abs(x)))


def _layer_norm_rows(v, g, b):
    mu = jnp.mean(v, axis=-1, keepdims=True)
    vc = v - mu
    var = jnp.mean(vc * vc, axis=-1, keepdims=True)
    return vc * lax.rsqrt(var + LN_EPS) * g + b


def _mm_kernel(x_ref, w_ref, o_ref, wb_ref):
    @pl.when(pl.program_id(1) == 0)
    def _():
        wb_ref[...] = w_ref[...].astype(BF16)

    o_ref[...] = jnp.dot(x_ref[...], wb_ref[...], preferred_element_type=F32).astype(o_ref.dtype)


def _matmul(x, w, out_dtype, tm, tn):
    m, k = x.shape
    n = w.shape[1]
    return pl.pallas_call(
        _mm_kernel,
        grid=(n // tn, m // tm),
        in_specs=[pl.BlockSpec((tm, k), lambda j, i: (i, 0)), pl.BlockSpec((k, tn), lambda j, i: (0, j))],
        out_specs=pl.BlockSpec((tm, tn), lambda j, i: (i, j)),
        out_shape=jax.ShapeDtypeStruct((m, n), out_dtype),
        scratch_shapes=[pltpu.VMEM((k, tn), BF16)],
        compiler_params=_cparams("arbitrary", "arbitrary"),
        name="matmul",
    )(x, w)


def _swiglu_kernel(x_ref, wg_ref, wu_ref, o_ref, wgb_ref, wub_ref):
    @pl.when(pl.program_id(1) == 0)
    def _():
        wgb_ref[...] = wg_ref[...].astype(BF16)
        wub_ref[...] = wu_ref[...].astype(BF16)

    x = x_ref[...]
    g = jnp.dot(x, wgb_ref[...], preferred_element_type=F32)
    u = jnp.dot(x, wub_ref[...], preferred_element_type=F32)
    o_ref[...] = (_silu(g) * u).astype(o_ref.dtype)


def _matmul_swiglu(x, w_gu, d_ff, tm, tn):
    m, k = x.shape
    nt = d_ff // tn
    return pl.pallas_call(
        _swiglu_kernel,
        grid=(nt, m // tm),
        in_specs=[pl.BlockSpec((tm, k), lambda j, i: (i, 0)),
                  pl.BlockSpec((k, tn), lambda j, i: (0, j)),
                  pl.BlockSpec((k, tn), lambda j, i: (0, j + nt))],
        out_specs=pl.BlockSpec((tm, tn), lambda j, i: (i, j)),
        out_shape=jax.ShapeDtypeStruct((m, d_ff), BF16),
        scratch_shapes=[pltpu.VMEM((k, tn), BF16), pltpu.VMEM((k, tn), BF16)],
        compiler_params=_cparams("arbitrary", "arbitrary"),
        name="matmul_swiglu",
    )(x, w_gu, w_gu)


def _mm_res_ln_kernel(h_ref, w_ref, x_ref, g_ref, b_ref, o_ref, ob_ref, wb_ref):
    @pl.when(pl.program_id(0) == 0)
    def _():
        wb_ref[...] = w_ref[...].astype(BF16)

    f = jnp.dot(h_ref[...], wb_ref[...], preferred_element_type=F32)
    y = _layer_norm_rows(DEEPNORM_ALPHA * x_ref[...] + f, g_ref[...], b_ref[...])
    o_ref[...] = y
    ob_ref[...] = y.astype(BF16)


def _matmul_res_ln(h, w, x, g, b, tm):
    m, k = h.shape
    d = w.shape[1]
    return pl.pallas_call(
        _mm_res_ln_kernel,
        grid=(m // tm,),
        in_specs=[pl.BlockSpec((tm, k), lambda i: (i, 0)), pl.BlockSpec((k, d), lambda i: (0, 0)),
                  pl.BlockSpec((tm, d), lambda i: (i, 0)),
                  pl.BlockSpec((1, d), lambda i: (0, 0)), pl.BlockSpec((1, d), lambda i: (0, 0))],
        out_specs=[pl.BlockSpec((tm, d), lambda i: (i, 0)), pl.BlockSpec((tm, d), lambda i: (i, 0))],
        out_shape=[jax.ShapeDtypeStruct((m, d), F32), jax.ShapeDtypeStruct((m, d), BF16)],
        scratch_shapes=[pltpu.VMEM((k, d), BF16)],
        compiler_params=_cparams("arbitrary"),
        name="matmul_res_ln",
    )(h, w, x, g.reshape(1, d), b.reshape(1, d))


def _add2_res_ln_kernel(y0_ref, y1_ref, gate_ref, x_ref, g_ref, b_ref, o_ref, ob_ref):
    f = gate_ref[:, 0:1] * y0_ref[...] + gate_ref[:, 1:2] * y1_ref[...]
    y = _layer_norm_rows(DEEPNORM_ALPHA * x_ref[...] + f, g_ref[...], b_ref[...])
    o_ref[...] = y
    ob_ref[...] = y.astype(BF16)


def _add2_res_ln(y0, y1, gate, x, g, b, tm):
    m, d = x.shape
    row = pl.BlockSpec((tm, d), lambda i: (i, 0))
    vec = pl.BlockSpec((1, d), lambda i: (0, 0))
    return pl.pallas_call(
        _add2_res_ln_kernel,
        grid=(m // tm,),
        in_specs=[row, row, pl.BlockSpec((tm, 128), lambda i: (i, 0)), row, vec, vec],
        out_specs=[row, row],
        out_shape=[jax.ShapeDtypeStruct((m, d), F32), jax.ShapeDtypeStruct((m, d), BF16)],
        compiler_params=_cparams("arbitrary"),
        name="moe_combine_res_ln",
    )(y0, y1, gate, x, g.reshape(1, d), b.reshape(1, d))


def _gdn_gates(ba, alog_row, dtb_row, valid):
    beta = _sigmoid(ba)
    g = -jnp.exp(alog_row) * _softplus(ba + dtb_row)
    if valid is not None:
        beta = jnp.where(valid, beta, 0.0)
        g = jnp.where(valid, g, 0.0)
    return beta, g


def _group_cumsum(g, group):
    r_in = lax.broadcasted_iota(jnp.int32, g.shape, 0) % group
    s = 1
    while s < group:
        g = g + jnp.where(r_in >= s, pltpu.roll(g, s, axis=0), 0.0)
        s *= 2
    return g


def _group_last(g, group):
    rows = g.shape[0]
    r_in = lax.broadcasted_iota(jnp.int32, g.shape, 0) % group
    x = jnp.where(r_in == group - 1, g, 0.0)
    s = 1
    while s < group:
        x = x + jnp.where(r_in + s < group, pltpu.roll(x, rows - s, axis=0), 0.0)
        s *= 2
    return x


def _l2norm_rows(t):
    return t * lax.rsqrt(jnp.sum(t * t, axis=-1, keepdims=True) + 1e-6)


def _wy_block(q, k, v, beta_c, g_c, g_r, g_end_c, incl, strict, levels):
    c = q.shape[0]
    decay = jnp.where(incl, jnp.exp(jnp.where(incl, g_c - g_r, 0.0)), 0.0)
    kb = k * beta_c
    vb = v * beta_c
    a = jnp.where(strict, _bdot_nt(kb, k) * decay, 0.0)
    eye = (lax.broadcasted_iota(jnp.int32, (c, c), 0) == lax.broadcasted_iota(jnp.int32, (c, c), 1)).astype(F32)
    n = -a
    t = eye + n
    for _ in range(levels - 1):
        n = _bdot(n, n)
        t = t + _bdot(t, n)
    eg = jnp.exp(g_c)
    uw = _bdot(t, jnp.concatenate([vb, kb * eg], axis=1))
    u, w = uw[:, :A_DV], uw[:, A_DV:]
    qk = jnp.where(incl, _bdot_nt(q, k) * decay, 0.0)
    qg = q * eg
    kdec = k * jnp.exp(g_end_c - g_c)
    return u, w, qk, qg, kdec


def _gate_out(o, z, norm_w):
    o = o * lax.rsqrt(jnp.mean(o * o, axis=-1, keepdims=True) + RMS_EPS) * norm_w
    return o * _silu(z)


def _conv_head_inputs(xbuf_ref, cw_ref, row0, rows, h):
    outs = []
    for part in range(3):
        c0 = part * A_HEADS * A_DK + h * A_DK
        acc = xbuf_ref[pl.ds(row0 - 3, rows), pl.ds(c0, A_DK)] * cw_ref[0:1, pl.ds(c0, A_DK)]
        for j in range(1, CONV_W):
            acc = acc + xbuf_ref[pl.ds(row0 - 3 + j, rows), pl.ds(c0, A_DK)] * cw_ref[j:j + 1, pl.ds(c0, A_DK)]
        outs.append(_silu(acc))
    q = _l2norm_rows(outs[0]) * (A_DK ** -0.5)
    k = _l2norm_rows(outs[1])
    return q, k, outs[2]


def _blockdiag(a, b):
    z = jnp.zeros_like(a)
    return jnp.concatenate([jnp.concatenate([a, z], axis=1), jnp.concatenate([z, b], axis=1)], axis=0)


def _gdn_prompt_kernel(proj_ref, cw_ref, alog_ref, dtb_ref, nw_ref, obuf_ref, o_ref, s_out_ref, c_out_ref,
                       xbuf_ref, s_ref):
    del obuf_ref
    i = pl.program_id(1)

    @pl.when(i == 0)
    def _():
        xbuf_ref[0:8, :] = jnp.zeros((8, CONV_DIM), F32)
        s_ref[...] = jnp.zeros(s_ref.shape, F32)

    pos = i * ROWS + lax.broadcasted_iota(jnp.int32, (ROWS, 1), 0)
    valid = pos >= LEAD
    xbuf_ref[8:8 + ROWS, :] = jnp.where(valid, proj_ref[:, 0:CONV_DIM], 0.0)

    beta, g = _gdn_gates(proj_ref[:, A_MAIN:A_PROJ_PAD], alog_ref[...], dtb_ref[...], valid)
    gcum = _group_cumsum(g, ROWS)
    gcum_t = gcum.T

    ri = lax.broadcasted_iota(jnp.int32, (ROWS, ROWS), 0)
    ci = lax.broadcasted_iota(jnp.int32, (ROWS, ROWS), 1)
    incl = ri >= ci
    strict = ri > ci

    a_s, qk_s, rhs_s, qg_s, kdec_s, gend_s = [], [], [], [], [], []
    for h in range(A_HEADS):
        q, k, v = _conv_head_inputs(xbuf_ref, cw_ref, 8, ROWS, h)
        g_c = gcum[:, 8 + h:9 + h]
        g_r = gcum_t[8 + h:9 + h, :]
        g_end = gcum[ROWS - 1:ROWS, 8 + h:9 + h]
        decay = jnp.where(incl, jnp.exp(jnp.where(incl, g_c - g_r, 0.0)), 0.0)
        bcol = beta[:, h:h + 1]
        kb = k * bcol
        eg = jnp.exp(g_c)
        kq = _bdot_nt(jnp.concatenate([kb, q], axis=0), k)
        a_s.append(jnp.where(strict, kq[:ROWS] * decay, 0.0))
        qk_s.append(jnp.where(incl, kq[ROWS:] * decay, 0.0).astype(BF16))
        rhs_s.append(jnp.concatenate([v * bcol, kb * eg], axis=1).astype(BF16))
        qg_s.append((q * eg).astype(BF16))
        kdec_s.append((k * jnp.exp(g_end - g_c)).astype(BF16))
        gend_s.append(g_end)

    r2 = lax.broadcasted_iota(jnp.int32, (2 * ROWS, 2 * ROWS), 0)
    c2 = lax.broadcasted_iota(jnp.int32, (2 * ROWS, 2 * ROWS), 1)
    eye = (r2 == c2).astype(F32)
    pairs = A_HEADS // 2
    a_bd = [_blockdiag(a_s[2 * p], a_s[2 * p + 1]) for p in range(pairs)]
    base = (r2 // INV_BASE) == (c2 // INV_BASE)
    nbf = [jnp.where(base, -a, 0.0).astype(BF16) for a in a_bd]
    t32 = [eye + n.astype(F32) for n in nbf]
    size = 2
    while size < INV_BASE:
        nbf = [jnp.dot(n, n, preferred_element_type=F32).astype(BF16) for n in nbf]
        t32 = [t + jnp.dot(t.astype(BF16), n, preferred_element_type=F32) for t, n in zip(t32, nbf)]
        size *= 2
    while size < ROWS:
        off = ((r2 // (2 * size)) == (c2 // (2 * size))) & ((r2 // size) != (c2 // size))
        tbf = [t.astype(BF16) for t in t32]
        xs = [jnp.dot(jnp.where(off, a, 0.0).astype(BF16), t, preferred_element_type=F32).astype(BF16)
              for a, t in zip(a_bd, tbf)]
        t32 = [t - jnp.dot(tb, x, preferred_element_type=F32) for t, tb, x in zip(t32, tbf, xs)]
        size *= 2
    uw_s = [jnp.dot(t32[p].astype(BF16), jnp.concatenate([rhs_s[2 * p], rhs_s[2 * p + 1]], axis=0),
                    preferred_element_type=F32) for p in range(pairs)]

    for h in range(A_HEADS):
        uw = uw_s[h // 2][(h % 2) * ROWS:(h % 2 + 1) * ROWS]
        u, w = uw[:, :A_DV], uw[:, A_DV:]
        s = s_ref[h]
        ws = jnp.dot(jnp.concatenate([w.astype(BF16), qg_s[h]], axis=0), s.astype(BF16), preferred_element_type=F32)
        v_new = (u - ws[:ROWS]).astype(BF16)
        o = ws[ROWS:] + jnp.dot(qk_s[h], v_new, preferred_element_type=F32)
        s_ref[h] = s * jnp.exp(gend_s[h]) + lax.dot_general(kdec_s[h], v_new, (((0,), (0,)), ((), ())),
                                                            preferred_element_type=F32)
        z = proj_ref[:, pl.ds(CONV_DIM + h * A_DV, A_DV)]
        o_ref[:, pl.ds(h * A_DV, A_DV)] = _gate_out(o, z, nw_ref[...]).astype(o_ref.dtype)

    xbuf_ref[0:8, :] = xbuf_ref[ROWS:ROWS + 8, :]

    @pl.when(i == pl.num_programs(1) - 1)
    def _():
        s_out_ref[0] = s_ref[...]
        c_out_ref[0] = xbuf_ref[ROWS:ROWS + 8, :]


def _gdn_prompt(proj, obuf, conv_w, alog_row, dtb_row, norm_w, n_batch, lp):
    nblk = lp // ROWS
    vec = lambda n: pl.BlockSpec((1, n), lambda b, i: (0, 0))
    return pl.pallas_call(
        _gdn_prompt_kernel,
        grid=(n_batch, nblk),
        in_specs=[pl.BlockSpec((ROWS, A_PROJ_PAD), lambda b, i: (b * nblk + i, 0)),
                  pl.BlockSpec((CONV_W, CONV_DIM), lambda b, i: (0, 0)),
                  vec(128), vec(128), vec(A_DV),
                  pl.BlockSpec(memory_space=pl.ANY)],
        out_specs=[pl.BlockSpec((ROWS, D_MODEL), lambda b, i: (b * nblk + i, 0)),
                   pl.BlockSpec((1, A_HEADS, A_DK, A_DV), lambda b, i: (b, 0, 0, 0)),
                   pl.BlockSpec((1, 8, CONV_DIM), lambda b, i: (b, 0, 0))],
        out_shape=[jax.ShapeDtypeStruct(obuf.shape, obuf.dtype),
                   jax.ShapeDtypeStruct((n_batch, A_HEADS, A_DK, A_DV), F32),
                   jax.ShapeDtypeStruct((n_batch, 8, CONV_DIM), F32)],
        scratch_shapes=[pltpu.VMEM((ROWS + 8, CONV_DIM), F32), pltpu.VMEM((A_HEADS, A_DK, A_DV), F32)],
        input_output_aliases={5: 0},
        compiler_params=_cparams("arbitrary", "arbitrary"),
        name="gdn_prompt",
    )(proj, conv_w, alog_row, dtb_row, norm_w.reshape(1, A_DV), obuf)


def _gdn_sample_kernel(proj_ref, c0_ref, s0_ref, cw_ref, alog_ref, dtb_ref, nw_ref, obuf_ref, o_ref, s_out_ref,
                       c_out_ref, xbuf_ref):
    del obuf_ref
    nb, steps = SAMPLE_BB, CHUNK // SAMPLE_BB
    for b in range(nb):
        xbuf_ref[pl.ds(b * 16 + 5, 3), :] = c0_ref[b]
        xbuf_ref[pl.ds(b * 16 + 8, steps), :] = proj_ref[pl.ds(b * steps, steps), 0:CONV_DIM]
        c_out_ref[b] = proj_ref[pl.ds((b + 1) * steps - (CONV_W - 1), CONV_W - 1), 0:CONV_DIM]

    beta, g = _gdn_gates(proj_ref[:, A_MAIN:A_PROJ_PAD], alog_ref[...], dtb_ref[...], None)
    gcum = _group_cumsum(g, steps)
    gcum_t = gcum.T
    gend = _group_last(gcum, steps)

    ri = lax.broadcasted_iota(jnp.int32, (CHUNK, CHUNK), 0)
    ci = lax.broadcasted_iota(jnp.int32, (CHUNK, CHUNK), 1)
    same = (ri // steps) == (ci // steps)
    incl = same & (ri >= ci)
    strict = same & (ri > ci)

    for h in range(A_HEADS):
        qs, ks, vs = [], [], []
        for b in range(nb):
            qb, kb_, vb_ = _conv_head_inputs(xbuf_ref, cw_ref, b * 16 + 8, steps, h)
            qs.append(qb), ks.append(kb_), vs.append(vb_)
        q, k, v = (jnp.concatenate(t, axis=0) for t in (qs, ks, vs))
        g_c = gcum[:, 8 + h:9 + h]
        g_r = gcum_t[8 + h:9 + h, :]
        g_e = gend[:, 8 + h:9 + h]
        u, w, qk, qg, kdec = _wy_block(q, k, v, beta[:, h:h + 1], g_c, g_r, g_e, incl, strict, 3)
        v_news, o_s = [], []
        for b in range(nb):
            r0, r1 = b * steps, (b + 1) * steps
            s = s0_ref[b, h]
            ws = _bdot(jnp.concatenate([w[r0:r1], qg[r0:r1]], axis=0), s)
            v_new = u[r0:r1] - ws[:steps]
            v_news.append(v_new)
            o_s.append(ws[steps:])
            s_out_ref[b, h] = s * jnp.exp(g_e[r1 - 1:r1, :]) + _bdot_tn(kdec[r0:r1], v_new)
        o = jnp.concatenate(o_s, axis=0) + _bdot(qk, jnp.concatenate(v_news, axis=0))
        z = proj_ref[:, pl.ds(CONV_DIM + h * A_DV, A_DV)]
        o_ref[:, pl.ds(h * A_DV, A_DV)] = _gate_out(o, z, nw_ref[...]).astype(o_ref.dtype)


def _gdn_sample(proj, obuf, conv0, s0, conv_w, alog_row, dtb_row, norm_w, row0, n_batch):
    blk0 = row0 // CHUNK
    vec = lambda n: pl.BlockSpec((1, n), lambda i: (0, 0))
    st = pl.BlockSpec((SAMPLE_BB, A_HEADS, A_DK, A_DV), lambda i: (i, 0, 0, 0))
    cs = pl.BlockSpec((SAMPLE_BB, CONV_W - 1, CONV_DIM), lambda i: (i, 0, 0))
    return pl.pallas_call(
        _gdn_sample_kernel,
        grid=(n_batch // SAMPLE_BB,),
        in_specs=[pl.BlockSpec((CHUNK, A_PROJ_PAD), lambda i: (blk0 + i, 0)),
                  cs, st,
                  pl.BlockSpec((CONV_W, CONV_DIM), lambda i: (0, 0)),
                  vec(128), vec(128), vec(A_DV),
                  pl.BlockSpec(memory_space=pl.ANY)],
        out_specs=[pl.BlockSpec((CHUNK, D_MODEL), lambda i: (blk0 + i, 0)), st, cs],
        out_shape=[jax.ShapeDtypeStruct(obuf.shape, obuf.dtype),
                   jax.ShapeDtypeStruct(s0.shape, F32),
                   jax.ShapeDtypeStruct(conv0.shape, F32)],
        scratch_shapes=[pltpu.VMEM((SAMPLE_BB * 16, CONV_DIM), F32)],
        input_output_aliases={7: 0},
        compiler_params=_cparams("arbitrary"),
        name="gdn_sample",
    )(proj, conv0, s0, conv_w, alog_row, dtb_row, norm_w.reshape(1, A_DV), obuf)


def _alibi_slope(h):
    return 2.0 ** (-8.0 * (h + 1) / B_HEADS)


def _sink_softmax_pv(pieces, sink):
    m = sink
    for s, _ in pieces:
        m = jnp.maximum(m, jnp.max(s, axis=-1, keepdims=True))
    den = jnp.exp(sink - m)
    acc = None
    ps = []
    for s, _ in pieces:
        p = jnp.exp(s - m)
        den = den + jnp.sum(p, axis=-1, keepdims=True)
        ps.append(p)
    inv = 1.0 / den
    for p, (_, v) in zip(ps, pieces):
        t = _bdot(p * inv, v)
        acc = t if acc is None else acc + t
    return acc


def _attn_prompt_kernel(sink_ref, q_ref, kvp_ref, kvc_ref, obuf_ref, o_ref):
    del obuf_ref
    i = pl.program_id(1)
    r = lax.broadcasted_iota(jnp.int32, (ROWS, 2 * ROWS), 0)
    c = lax.broadcasted_iota(jnp.int32, (ROWS, 2 * ROWS), 1)
    dist = r - c + ROWS
    kpos = i * ROWS - ROWS - LEAD + c
    valid = (kpos >= 0) & (dist >= 0) & (dist <= WINDOW)
    dist_f = dist.astype(F32)
    for g in range(B_KV_HEADS):
        k = jnp.concatenate([kvp_ref[:, pl.ds(g * B_HD, B_HD)], kvc_ref[:, pl.ds(g * B_HD, B_HD)]], axis=0)
        v = jnp.concatenate([kvp_ref[:, pl.ds(B_KV + g * B_HD, B_HD)], kvc_ref[:, pl.ds(B_KV + g * B_HD, B_HD)]], axis=0)
        kb, vb = k.astype(BF16), v.astype(BF16)
        for j in range(B_GROUP):
            h = g * B_GROUP + j
            s = _bdot_nt(q_ref[:, pl.ds(h * B_HD, B_HD)], kb) * (B_HD ** -0.5)
            s = jnp.where(valid, s - _alibi_slope(h) * dist_f, NEG_INF)
            o = _sink_softmax_pv([(s, vb)], sink_ref[h])
            o_ref[:, pl.ds(h * B_HD, B_HD)] = o.astype(o_ref.dtype)


def _attn_prompt(q, kv, obuf, sinks, n_batch, lp):
    nblk = lp // ROWS
    return pl.pallas_call(
        _attn_prompt_kernel,
        grid_spec=pltpu.PrefetchScalarGridSpec(
            num_scalar_prefetch=1,
            grid=(n_batch, nblk),
            in_specs=[pl.BlockSpec((ROWS, D_MODEL), lambda b, i, s: (b * nblk + i, 0)),
                      pl.BlockSpec((ROWS, 2 * B_KV), lambda b, i, s: (b * nblk + jnp.maximum(i - 1, 0), 0)),
                      pl.BlockSpec((ROWS, 2 * B_KV), lambda b, i, s: (b * nblk + i, 0)),
                      pl.BlockSpec(memory_space=pl.ANY)],
            out_specs=pl.BlockSpec((ROWS, D_MODEL), lambda b, i, s: (b * nblk + i, 0)),
        ),
        out_shape=jax.ShapeDtypeStruct(obuf.shape, obuf.dtype),
        input_output_aliases={4: 0},
        compiler_params=_cparams("arbitrary", "arbitrary"),
        name="attn_prompt",
    )(sinks, q, kv, kv, obuf)


ATT_BB = 8
ATT_S = 8


def _attn_sample_kernel(sink_ref, q_ref, kvn_ref, ck_ref, cv_ref, obuf_ref, o_ref):
    del obuf_ref
    bt = ATT_BB * ATT_S
    rows = B_GROUP * bt
    r = lax.broadcasted_iota(jnp.int32, (rows, ATT_BB * WINDOW), 0)
    c = lax.broadcasted_iota(jnp.int32, (rows, ATT_BB * WINDOW), 1)
    dist_c = WINDOW + r % ATT_S - c % WINDOW
    valid_c = ((r % bt) // ATT_S == c // WINDOW) & (dist_c <= WINDOW)
    dist_cf = dist_c.astype(F32)
    rn = lax.broadcasted_iota(jnp.int32, (rows, bt), 0)
    cn = lax.broadcasted_iota(jnp.int32, (rows, bt), 1)
    dist_n = rn % ATT_S - cn % ATT_S
    valid_n = ((rn % bt) // ATT_S == cn // ATT_S) & (dist_n >= 0)
    dist_nf = dist_n.astype(F32)
    hrow = lax.broadcasted_iota(jnp.int32, (rows, 1), 0) // bt
    for g in range(B_KV_HEADS):
        slope = jnp.zeros((rows, 1), F32)
        sink = jnp.zeros((rows, 1), F32)
        for j in range(B_GROUP):
            slope = jnp.where(hrow == j, _alibi_slope(g * B_GROUP + j), slope)
            sink = jnp.where(hrow == j, sink_ref[g * B_GROUP + j], sink)
        q = jnp.concatenate([q_ref[:, pl.ds((g * B_GROUP + j) * B_HD, B_HD)] for j in range(B_GROUP)], axis=0)
        kc = jnp.concatenate([ck_ref[b, :, pl.ds(g * B_HD, B_HD)] for b in range(ATT_BB)], axis=0)
        vc = jnp.concatenate([cv_ref[b, :, pl.ds(g * B_HD, B_HD)] for b in range(ATT_BB)], axis=0)
        kn = kvn_ref[:, pl.ds(g * B_HD, B_HD)]
        vn = kvn_ref[:, pl.ds(B_KV + g * B_HD, B_HD)]
        s_c = _bdot_nt(q, kc) * (B_HD ** -0.5)
        s_c = jnp.where(valid_c, s_c - slope * dist_cf, NEG_INF)
        s_n = _bdot_nt(q, kn) * (B_HD ** -0.5)
        s_n = jnp.where(valid_n, s_n - slope * dist_nf, NEG_INF)
        o = _sink_softmax_pv([(s_c, vc), (s_n, vn)], sink)
        for j in range(B_GROUP):
            h = g * B_GROUP + j
            o_ref[:, pl.ds(h * B_HD, B_HD)] = o[j * bt:(j + 1) * bt].astype(o_ref.dtype)


def _attn_sample(q, kv, obuf, cache_k, cache_v, sinks, row0, n_batch):
    rows = ATT_BB * ATT_S
    blk0 = row0 // rows
    cache = pl.BlockSpec((ATT_BB, WINDOW, B_KV), lambda i, s: (i, 0, 0))
    return pl.pallas_call(
        _attn_sample_kernel,
        grid_spec=pltpu.PrefetchScalarGridSpec(
            num_scalar_prefetch=1,
            grid=(n_batch // ATT_BB,),
            in_specs=[pl.BlockSpec((rows, D_MODEL), lambda i, s: (blk0 + i, 0)),
                      pl.BlockSpec((rows, 2 * B_KV), lambda i, s: (blk0 + i, 0)),
                      cache, cache,
                      pl.BlockSpec(memory_space=pl.ANY)],
            out_specs=pl.BlockSpec((rows, D_MODEL), lambda i, s: (blk0 + i, 0)),
        ),
        out_shape=jax.ShapeDtypeStruct(obuf.shape, obuf.dtype),
        input_output_aliases={5: 0},
        compiler_params=_cparams("arbitrary"),
        name="attn_sample",
    )(sinks, q, kv, cache_k, cache_v, obuf)


def _split2(x):
    hi = x.astype(BF16)
    lo = (x - hi.astype(F32)).astype(BF16)
    return hi, lo


def _router_kernel(x_ref, w_ref, b_ref, idx_ref, gate_ref, rank_ref, cnt_ref, run_ref):
    @pl.when(pl.program_id(0) == 0)
    def _():
        run_ref[...] = jnp.zeros(run_ref.shape, F32)

    xs = _split2(x_ref[...])
    ws = _split2(w_ref[...])
    logits = b_ref[...]
    for i, j in ((1, 1), (0, 1), (1, 0), (0, 0)):
        logits = logits + jnp.dot(xs[i], ws[j], preferred_element_type=F32)
    lane = lax.broadcasted_iota(jnp.int32, logits.shape, 1)
    real = lane < N_EXPERTS
    logits = jnp.where(real, logits, NEG_INF)
    m = jnp.max(logits, axis=-1, keepdims=True)
    e = jnp.where(real, jnp.exp(logits - m), 0.0)
    probs = e / jnp.sum(e, axis=-1, keepdims=True)
    p1 = jnp.max(probs, axis=-1, keepdims=True)
    i1 = jnp.min(jnp.where(probs == p1, lane, 128), axis=-1, keepdims=True)
    rest = jnp.where((lane == i1) | (~real), -1.0, probs)
    p2 = jnp.max(rest, axis=-1, keepdims=True)
    i2 = jnp.min(jnp.where(rest == p2, lane, 128), axis=-1, keepdims=True)
    tot = p1 + p2
    idx_ref[...] = jnp.where(lane == 0, i1, jnp.where(lane == 1, i2, 0))
    gate_ref[...] = jnp.where(lane == 0, p1 / tot, jnp.where(lane == 1, p2 / tot, 0.0))

    tm = logits.shape[0]
    chosen = ((lane == i1) | (lane == i2)).astype(F32)
    earlier = (lax.broadcasted_iota(jnp.int32, (tm, tm), 0) > lax.broadcasted_iota(jnp.int32, (tm, tm), 1))
    before = jnp.dot(earlier.astype(BF16), chosen.astype(BF16), preferred_element_type=F32) + run_ref[...]
    r1 = jnp.sum(jnp.where(lane == i1, before, 0.0), axis=-1, keepdims=True)
    r2 = jnp.sum(jnp.where(lane == i2, before, 0.0), axis=-1, keepdims=True)
    rank_ref[...] = jnp.where(lane == 0, r1, jnp.where(lane == 1, r2, 0.0)).astype(jnp.int32)
    run_ref[...] = run_ref[...] + jnp.sum(chosen, axis=0, keepdims=True)
    cnt_ref[...] = run_ref[...]


def _router(x, w_router, b_router, tm):
    m, d = x.shape
    w = jnp.zeros((d, 128), F32).at[:, :N_EXPERTS].set(w_router)
    b = jnp.zeros((1, 128), F32).at[0, :N_EXPERTS].set(b_router)
    out = pl.BlockSpec((tm, 128), lambda i: (i, 0))
    one = pl.BlockSpec((1, 128), lambda i: (0, 0))
    return pl.pallas_call(
        _router_kernel,
        grid=(m // tm,),
        in_specs=[pl.BlockSpec((tm, d), lambda i: (i, 0)), pl.BlockSpec((d, 128), lambda i: (0, 0)), one],
        out_specs=[out, out, out, one],
        out_shape=[jax.ShapeDtypeStruct((m, 128), jnp.int32), jax.ShapeDtypeStruct((m, 128), F32),
                   jax.ShapeDtypeStruct((m, 128), jnp.int32), jax.ShapeDtypeStruct((1, 128), F32)],
        scratch_shapes=[pltpu.VMEM((1, 128), F32)],
        compiler_params=_cparams("arbitrary"),
        name="moe_router",
    )(x, w, b)


def _moe_ffn_kernel(te_ref, xs_ref, wgu_ref, wd_ref, o_ref, wgub_ref, wdb_ref):
    i = pl.program_id(0)

    @pl.when((i == 0) | (te_ref[i] != te_ref[jnp.maximum(i - 1, 0)]))
    def _():
        wgub_ref[...] = wgu_ref[0].astype(BF16)
        wdb_ref[...] = wd_ref[0].astype(BF16)

    d_e = wd_ref.shape[1]
    gu = jnp.dot(xs_ref[...].astype(BF16), wgub_ref[...], preferred_element_type=F32)
    h = (_silu(gu[:, :d_e]) * gu[:, d_e:]).astype(BF16)
    o_ref[...] = jnp.dot(h, wdb_ref[...], preferred_element_type=F32)


def _moe_ffn(tile_expert, xs, w_gu, w_down):
    a_pad, d = xs.shape
    d_e = w_down.shape[1]
    return pl.pallas_call(
        _moe_ffn_kernel,
        grid_spec=pltpu.PrefetchScalarGridSpec(
            num_scalar_prefetch=1,
            grid=(a_pad // MOE_TM,),
            in_specs=[pl.BlockSpec((MOE_TM, d), lambda i, te: (i, 0)),
                      pl.BlockSpec((1, d, 2 * d_e), lambda i, te: (te[i], 0, 0)),
                      pl.BlockSpec((1, d_e, d), lambda i, te: (te[i], 0, 0))],
            out_specs=pl.BlockSpec((MOE_TM, d), lambda i, te: (i, 0)),
            scratch_shapes=[pltpu.VMEM((d, 2 * d_e), BF16), pltpu.VMEM((d_e, d), BF16)],
        ),
        out_shape=jax.ShapeDtypeStruct((a_pad, d), F32),
        compiler_params=_cparams("arbitrary"),
        name="moe_experts",
    )(tile_expert, xs, w_gu, w_down)


def _moe_layer(x, xb, w_router, b_router, w_gu, w_down, layer, ln_g, ln_b, tm):
    tt = x.shape[0]
    idx, gate, rank, cnt = _router(x, w_router, b_router, tm)
    counts = cnt[0, :N_EXPERTS].astype(jnp.int32)
    padded = ((counts + MOE_TM - 1) // MOE_TM) * MOE_TM
    pend = jnp.cumsum(padded)
    pstart = pend - padded
    dest_l = rank
    for e in range(N_EXPERTS):
        dest_l = dest_l + jnp.where(idx == e, pstart[e], 0)
    dest = dest_l[:, :2].reshape(-1)
    a_pad = 2 * tt + N_EXPERTS * MOE_TM
    src_tok = jnp.zeros((a_pad,), jnp.int32).at[dest].set(jnp.arange(2 * tt, dtype=jnp.int32) // 2,
                                                          unique_indices=True, mode="promise_in_bounds")
    tile_start = jnp.arange(a_pad // MOE_TM, dtype=jnp.int32) * MOE_TM
    tile_expert = jnp.minimum(jnp.sum(tile_start[:, None] >= pend[None, :], axis=1), N_EXPERTS - 1).astype(jnp.int32)
    del xb
    xs = jnp.take(x, src_tok, axis=0, mode="clip")
    ys = _moe_ffn(tile_expert + layer * N_EXPERTS, xs, w_gu, w_down)
    dest2 = dest.reshape(tt, 2)
    y0 = jnp.take(ys, dest2[:, 0], axis=0, mode="clip")
    y1 = jnp.take(ys, dest2[:, 1], axis=0, mode="clip")
    return _add2_res_ln(y0, y1, gate, x, ln_g, ln_b, tm)


def kernel(x_prompt, x_sample, state_gdn, state_conv, cache_win_k, cache_win_v, meta_tokens, a_w_in, a_conv_w, a_A_log, a_dt_bias, a_norm_w, a_w_out, b_w_kv, b_w_q, b_sinks, b_w_o, ln_mix_g, ln_mix_b, ln_ffn_g, ln_ffn_b, ffd_w_gu, ffd_w_down, moe_w_router, moe_b_router, moe_w_gu, moe_w_down):
    nb, seq, d = x_prompt.shape
    nsb, steps, _ = x_sample.shape
    lp = LEAD + N_META + seq
    n_prompt = nb * lp
    tt = n_prompt + nsb * steps
    tm = 384 if tt % 384 == 0 else 128
    d_ff = ffd_w_down.shape[1]

    pieces = []
    for b in range(nb):
        pieces += [jnp.zeros((LEAD, d), F32), meta_tokens.astype(F32), x_prompt[b]]
    x = jnp.concatenate(pieces + [x_sample.reshape(nsb * steps, d)], axis=0)
    xb = x.astype(BF16)

    p_gdn, p_conv, s_gdn, s_conv = [], [], [], []
    kv = None
    n_moe, n_exp = moe_w_gu.shape[:2]
    moe_gu_all = moe_w_gu.reshape(n_moe * n_exp, *moe_w_gu.shape[2:])
    moe_down_all = moe_w_down.reshape(n_moe * n_exp, *moe_w_down.shape[2:])
    mix_in = jnp.zeros((tt, d), BF16)
    for l in range(DEPTH):
        if l < N_A_LAYERS:
            w_in = jnp.concatenate([a_w_in[l], jnp.zeros((d, A_PROJ_PAD - a_w_in.shape[2]), F32)], axis=1)
            proj = _matmul(xb, w_in, F32, tm, A_PROJ_PAD // 3)
            alog_row = jnp.zeros((1, 128), F32).at[0, 8:16].set(a_A_log[l])
            dtb_row = jnp.zeros((1, 128), F32).at[0, 8:16].set(a_dt_bias[l])
            mix_in, s_p, c_p = _gdn_prompt(proj, mix_in, a_conv_w[l], alog_row, dtb_row, a_norm_w[l], nb, lp)
            mix_in, s_s, c_s = _gdn_sample(proj, mix_in, state_conv[l], state_gdn[l], a_conv_w[l], alog_row, dtb_row,
                                           a_norm_w[l], n_prompt, nsb)
            p_gdn.append(s_p)
            s_gdn.append(s_s)
            p_conv.append(c_p[:, 8 - (CONV_W - 1):])
            s_conv.append(c_s)
            w_mix = a_w_out[l]
        else:
            j = l - N_A_LAYERS
            if j == 0:
                kv = _matmul(xb, b_w_kv, F32, tm, 2 * B_KV)
            q = _matmul(xb, b_w_q[j], BF16, tm, d)
            mix_in = _attn_prompt(q, kv, mix_in, b_sinks[j], nb, lp)
            mix_in = _attn_sample(q, kv, mix_in, cache_win_k.reshape(nsb, WINDOW, B_KV),
                                  cache_win_v.reshape(nsb, WINDOW, B_KV), b_sinks[j], n_prompt, nsb)
            w_mix = b_w_o[j]
        x, xb = _matmul_res_ln(mix_in, w_mix, x, ln_mix_g[l], ln_mix_b[l], tm)
        if l % 2 == 0:
            hmid = _matmul_swiglu(xb, ffd_w_gu[l // 2], d_ff, tm, d_ff // 2)
            x, xb = _matmul_res_ln(hmid, ffd_w_down[l // 2], x, ln_ffn_g[l], ln_ffn_b[l], tm)
        else:
            x, xb = _moe_layer(x, xb, moe_w_router[l // 2], moe_b_router[l // 2], moe_gu_all, moe_down_all, l // 2,
                               ln_ffn_g[l], ln_ffn_b[l], tm)

    y_prompt = x[:n_prompt].reshape(nb, lp, d)[:, LEAD + N_META:]
    y_sample = x[n_prompt:].reshape(nsb, steps, d)
    kvp = jnp.stack([kv[(b + 1) * lp - WINDOW:(b + 1) * lp] for b in range(nb)]).reshape(nb, WINDOW, 2, B_KV_HEADS, B_HD)
    kvs = kv[n_prompt:].reshape(nsb, steps, 2, B_KV_HEADS, B_HD)
    s_wk = jnp.concatenate([cache_win_k[:, steps:], kvs[:, :, 0]], axis=1)
    s_wv = jnp.concatenate([cache_win_v[:, steps:], kvs[:, :, 1]], axis=1)
    return (y_prompt, y_sample, jnp.stack(p_gdn), jnp.stack(p_conv), kvp[:, :, 0], kvp[:, :, 1],
            jnp.stack(s_gdn), jnp.stack(s_conv), s_wk, s_wv)
```

```python
import functools
import math

import jax
import jax.numpy as jnp
from jax import lax
from jax.experimental import pallas as pl
from jax.experimental.pallas import tpu as pltpu

F32 = jnp.float32
BF16 = jnp.bfloat16

D_MODEL = 1024
N_META = 16
A_HEADS = 8
A_DK = 128
A_DV = 128
CONV_W = 4
CONV_DIM = A_HEADS * (2 * A_DK + A_DV)
A_MAIN = CONV_DIM + A_HEADS * A_DV
A_PROJ_PAD = A_MAIN + 128
B_HD = 64
B_HEADS = 16
B_KV_HEADS = 4
B_GROUP = 4
B_KV = B_KV_HEADS * B_HD
WINDOW = 128
N_EXPERTS = 8
DEPTH = 4
N_A_LAYERS = 2
DEEPNORM_ALPHA = (2 * DEPTH) ** 0.25
LN_EPS = 1e-5
RMS_EPS = 1e-6
NEG_INF = -1e30

ROWS = 128
CHUNK = 64
INV_BASE = 32
LEAD = (-N_META) % ROWS
SAMPLE_BB = CHUNK // 8
MOE_TM = 256
VMEM_LIMIT = 56 * 1024 * 1024


def _cparams(*sem):
    return pltpu.CompilerParams(dimension_semantics=sem, vmem_limit_bytes=VMEM_LIMIT)


def _bdot(a, b):
    return jnp.dot(a.astype(BF16), b.astype(BF16), preferred_element_type=F32)


def _bdot_nt(a, b):
    return lax.dot_general(a.astype(BF16), b.astype(BF16), (((1,), (1,)), ((), ())), preferred_element_type=F32)


def _bdot_tn(a, b):
    return lax.dot_general(a.astype(BF16), b.astype(BF16), (((0,), (0,)), ((), ())), preferred_element_type=F32)


def _sigmoid(x):
    return 1.0 / (1.0 + jnp.exp(-x))


def _silu(x):
    return x * _sigmoid(x)


def _softplus(x):
    return jnp.maximum(x, 0.0) + jnp.log(1.0 + jnp.exp(-jnp.abs(x)))


def _layer_norm_rows(v, g, b):
    mu = jnp.mean(v, axis=-1, keepdims=True)
    vc = v - mu
    var = jnp.mean(vc * vc, axis=-1, keepdims=True)
    return vc * lax.rsqrt(var + LN_EPS) * g + b


def _mm_kernel(x_ref, w_ref, o_ref, wb_ref):
    @pl.when(pl.program_id(1) == 0)
    def _():
        wb_ref[...] = w_ref[...].astype(BF16)

    o_ref[...] = jnp.dot(x_ref[...], wb_ref[...], preferred_element_type=F32).astype(o_ref.dtype)


def _matmul(x, w, out_dtype, tm, tn):
    m, k = x.shape
    n = w.shape[1]
    return pl.pallas_call(
        _mm_kernel,
        grid=(n // tn, m // tm),
        in_specs=[pl.BlockSpec((tm, k), lambda j, i: (i, 0)), pl.BlockSpec((k, tn), lambda j, i: (0, j))],
        out_specs=pl.BlockSpec((tm, tn), lambda j, i: (i, j)),
        out_shape=jax.ShapeDtypeStruct((m, n), out_dtype),
        scratch_shapes=[pltpu.VMEM((k, tn), BF16)],
        compiler_params=_cparams("arbitrary", "arbitrary"),
        name="matmul",
    )(x, w)


def _swiglu_kernel(x_ref, wg_ref, wu_ref, o_ref, wgb_ref, wub_ref):
    @pl.when(pl.program_id(1) == 0)
    def _():
        wgb_ref[...] = wg_ref[...].astype(BF16)
        wub_ref[...] = wu_ref[...].astype(BF16)

    x = x_ref[...]
    g = jnp.dot(x, wgb_ref[...], preferred_element_type=F32)
    u = jnp.dot(x, wub_ref[...], preferred_element_type=F32)
    o_ref[...] = (_silu(g) * u).astype(o_ref.dtype)


def _matmul_swiglu(x, w_gu, d_ff, tm, tn):
    m, k = x.shape
    nt = d_ff // tn
    return pl.pallas_call(
        _swiglu_kernel,
        grid=(nt, m // tm),
        in_specs=[pl.BlockSpec((tm, k), lambda j, i: (i, 0)),
                  pl.BlockSpec((k, tn), lambda j, i: (0, j)),
                  pl.BlockSpec((k, tn), lambda j, i: (0, j + nt))],
        out_specs=pl.BlockSpec((tm, tn), lambda j, i: (i, j)),
        out_shape=jax.ShapeDtypeStruct((m, d_ff), BF16),
        scratch_shapes=[pltpu.VMEM((k, tn), BF16), pltpu.VMEM((k, tn), BF16)],
        compiler_params=_cparams("arbitrary", "arbitrary"),
        name="matmul_swiglu",
    )(x, w_gu, w_gu)


def _mm_res_ln_kernel(h_ref, w_ref, x_ref, g_ref, b_ref, o_ref, ob_ref, wb_ref):
    @pl.when(pl.program_id(0) == 0)
    def _():
        wb_ref[...] = w_ref[...].astype(BF16)

    f = jnp.dot(h_ref[...], wb_ref[...], preferred_element_type=F32)
    y = _layer_norm_rows(DEEPNORM_ALPHA * x_ref[...] + f, g_ref[...], b_ref[...])
    o_ref[...] = y
    ob_ref[...] = y.astype(BF16)


def _matmul_res_ln(h, w, x, g, b, tm):
    m, k = h.shape
    d = w.shape[1]
    return pl.pallas_call(
        _mm_res_ln_kernel,
        grid=(m // tm,),
        in_specs=[pl.BlockSpec((tm, k), lambda i: (i, 0)), pl.BlockSpec((k, d), lambda i: (0, 0)),
                  pl.BlockSpec((tm, d), lambda i: (i, 0)),
                  pl.BlockSpec((1, d), lambda i: (0, 0)), pl.BlockSpec((1, d), lambda i: (0, 0))],
        out_specs=[pl.BlockSpec((tm, d), lambda i: (i, 0)), pl.BlockSpec((tm, d), lambda i: (i, 0))],
        out_shape=[jax.ShapeDtypeStruct((m, d), F32), jax.ShapeDtypeStruct((m, d), BF16)],
        scratch_shapes=[pltpu.VMEM((k, d), BF16)],
        compiler_params=_cparams("arbitrary"),
        name="matmul_res_ln",
    )(h, w, x, g.reshape(1, d), b.reshape(1, d))


def _add2_res_ln_kernel(y0_ref, y1_ref, gate_ref, x_ref, g_ref, b_ref, o_ref, ob_ref):
    f = gate_ref[:, 0:1] * y0_ref[...] + gate_ref[:, 1:2] * y1_ref[...]
    y = _layer_norm_rows(DEEPNORM_ALPHA * x_ref[...] + f, g_ref[...], b_ref[...])
    o_ref[...] = y
    ob_ref[...] = y.astype(BF16)


def _add2_res_ln(y0, y1, gate, x, g, b, tm):
    m, d = x.shape
    row = pl.BlockSpec((tm, d), lambda i: (i, 0))
    vec = pl.BlockSpec((1, d), lambda i: (0, 0))
    return pl.pallas_call(
        _add2_res_ln_kernel,
        grid=(m // tm,),
        in_specs=[row, row, pl.BlockSpec((tm, 128), lambda i: (i, 0)), row, vec, vec],
        out_specs=[row, row],
        out_shape=[jax.ShapeDtypeStruct((m, d), F32), jax.ShapeDtypeStruct((m, d), BF16)],
        compiler_params=_cparams("arbitrary"),
        name="moe_combine_res_ln",
    )(y0, y1, gate, x, g.reshape(1, d), b.reshape(1, d))


def _gdn_gates(ba, alog_row, dtb_row, valid):
    beta = _sigmoid(ba)
    g = -jnp.exp(alog_row) * _softplus(ba + dtb_row)
    if valid is not None:
        beta = jnp.where(valid, beta, 0.0)
        g = jnp.where(valid, g, 0.0)
    return beta, g


def _group_cumsum(g, group):
    r_in = lax.broadcasted_iota(jnp.int32, g.shape, 0) % group
    s = 1
    while s < group:
        g = g + jnp.where(r_in >= s, pltpu.roll(g, s, axis=0), 0.0)
        s *= 2
    return g


def _group_last(g, group):
    rows = g.shape[0]
    r_in = lax.broadcasted_iota(jnp.int32, g.shape, 0) % group
    x = jnp.where(r_in == group - 1, g, 0.0)
    s = 1
    while s < group:
        x = x + jnp.where(r_in + s < group, pltpu.roll(x, rows - s, axis=0), 0.0)
        s *= 2
    return x


def _l2norm_rows(t):
    return t * lax.rsqrt(jnp.sum(t * t, axis=-1, keepdims=True) + 1e-6)


def _wy_block(q, k, v, beta_c, g_c, g_r, g_end_c, incl, strict, levels):
    c = q.shape[0]
    decay = jnp.where(incl, jnp.exp(jnp.where(incl, g_c - g_r, 0.0)), 0.0)
    kb = k * beta_c
    vb = v * beta_c
    a = jnp.where(strict, _bdot_nt(kb, k) * decay, 0.0)
    eye = (lax.broadcasted_iota(jnp.int32, (c, c), 0) == lax.broadcasted_iota(jnp.int32, (c, c), 1)).astype(F32)
    n = -a
    t = eye + n
    for _ in range(levels - 1):
        n = _bdot(n, n)
        t = t + _bdot(t, n)
    eg = jnp.exp(g_c)
    uw = _bdot(t, jnp.concatenate([vb, kb * eg], axis=1))
    u, w = uw[:, :A_DV], uw[:, A_DV:]
    qk = jnp.where(incl, _bdot_nt(q, k) * decay, 0.0)
    qg = q * eg
    kdec = k * jnp.exp(g_end_c - g_c)
    return u, w, qk, qg, kdec


def _gate_out(o, z, norm_w):
    o = o * lax.rsqrt(jnp.mean(o * o, axis=-1, keepdims=True) + RMS_EPS) * norm_w
    return o * _silu(z)


def _conv_head_inputs(xbuf_ref, cw_ref, row0, rows, h):
    outs = []
    for part in range(3):
        c0 = part * A_HEADS * A_DK + h * A_DK
        acc = xbuf_ref[pl.ds(row0 - 3, rows), pl.ds(c0, A_DK)] * cw_ref[0:1, pl.ds(c0, A_DK)]
        for j in range(1, CONV_W):
            acc = acc + xbuf_ref[pl.ds(row0 - 3 + j, rows), pl.ds(c0, A_DK)] * cw_ref[j:j + 1, pl.ds(c0, A_DK)]
        outs.append(_silu(acc))
    q = _l2norm_rows(outs[0]) * (A_DK ** -0.5)
    k = _l2norm_rows(outs[1])
    return q, k, outs[2]


def _blockdiag(a, b):
    z = jnp.zeros_like(a)
    return jnp.concatenate([jnp.concatenate([a, z], axis=1), jnp.concatenate([z, b], axis=1)], axis=0)


def _gdn_prompt_kernel(proj_ref, cw_ref, alog_ref, dtb_ref, nw_ref, obuf_ref, o_ref, s_out_ref, c_out_ref,
                       xbuf_ref, s_ref):
    del obuf_ref
    i = pl.program_id(1)

    @pl.when(i == 0)
    def _():
        xbuf_ref[0:8, :] = jnp.zeros((8, CONV_DIM), F32)
        s_ref[...] = jnp.zeros(s_ref.shape, F32)

    pos = i * ROWS + lax.broadcasted_iota(jnp.int32, (ROWS, 1), 0)
    valid = pos >= LEAD
    xbuf_ref[8:8 + ROWS, :] = jnp.where(valid, proj_ref[:, 0:CONV_DIM], 0.0)

    beta, g = _gdn_gates(proj_ref[:, A_MAIN:A_PROJ_PAD], alog_ref[...], dtb_ref[...], valid)
    gcum = _group_cumsum(g, ROWS)
    gcum_t = gcum.T

    ri = lax.broadcasted_iota(jnp.int32, (ROWS, ROWS), 0)
    ci = lax.broadcasted_iota(jnp.int32, (ROWS, ROWS), 1)
    incl = ri >= ci
    strict = ri > ci

    a_s, qk_s, rhs_s, qg_s, kdec_s, gend_s = [], [], [], [], [], []
    for h in range(A_HEADS):
        q, k, v = _conv_head_inputs(xbuf_ref, cw_ref, 8, ROWS, h)
        g_c = gcum[:, 8 + h:9 + h]
        g_r = gcum_t[8 + h:9 + h, :]
        g_end = gcum[ROWS - 1:ROWS, 8 + h:9 + h]
        decay = jnp.where(incl, jnp.exp(jnp.where(incl, g_c - g_r, 0.0)), 0.0)
        bcol = beta[:, h:h + 1]
        kb = k * bcol
        eg = jnp.exp(g_c)
        kq = _bdot_nt(jnp.concatenate([kb, q], axis=0), k)
        a_s.append(jnp.where(strict, kq[:ROWS] * decay, 0.0))
        qk_s.append(jnp.where(incl, kq[ROWS:] * decay, 0.0).astype(BF16))
        rhs_s.append(jnp.concatenate([v * bcol, kb * eg], axis=1).astype(BF16))
        qg_s.append((q * eg).astype(BF16))
        kdec_s.append((k * jnp.exp(g_end - g_c)).astype(BF16))
        gend_s.append(g_end)

    r2 = lax.broadcasted_iota(jnp.int32, (2 * ROWS, 2 * ROWS), 0)
    c2 = lax.broadcasted_iota(jnp.int32, (2 * ROWS, 2 * ROWS), 1)
    eye = (r2 == c2).astype(F32)
    pairs = A_HEADS // 2
    a_bd = [_blockdiag(a_s[2 * p], a_s[2 * p + 1]) for p in range(pairs)]
    base = (r2 // INV_BASE) == (c2 // INV_BASE)
    nbf = [jnp.where(base, -a, 0.0).astype(BF16) for a in a_bd]
    t32 = [eye + n.astype(F32) for n in nbf]
    size = 2
    while size < INV_BASE:
        nbf = [jnp.dot(n, n, preferred_element_type=F32).astype(BF16) for n in nbf]
        t32 = [t + jnp.dot(t.astype(BF16), n, preferred_element_type=F32) for t, n in zip(t32, nbf)]
        size *= 2
    while size < ROWS:
        off = ((r2 // (2 * size)) == (c2 // (2 * size))) & ((r2 // size) != (c2 // size))
        tbf = [t.astype(BF16) for t in t32]
        xs = [jnp.dot(jnp.where(off, a, 0.0).astype(BF16), t, preferred_element_type=F32).astype(BF16)
              for a, t in zip(a_bd, tbf)]
        t32 = [t - jnp.dot(tb, x, preferred_element_type=F32) for t, tb, x in zip(t32, tbf, xs)]
        size *= 2
    uw_s = [jnp.dot(t32[p].astype(BF16), jnp.concatenate([rhs_s[2 * p], rhs_s[2 * p + 1]], axis=0),
                    preferred_element_type=F32) for p in range(pairs)]

    for h in range(A_HEADS):
        uw = uw_s[h // 2][(h % 2) * ROWS:(h % 2 + 1) * ROWS]
        u, w = uw[:, :A_DV], uw[:, A_DV:]
        s = s_ref[h]
        ws = jnp.dot(jnp.concatenate([w.astype(BF16), qg_s[h]], axis=0), s.astype(BF16), preferred_element_type=F32)
        v_new = (u - ws[:ROWS]).astype(BF16)
        o = ws[ROWS:] + jnp.dot(qk_s[h], v_new, preferred_element_type=F32)
        s_ref[h] = s * jnp.exp(gend_s[h]) + lax.dot_general(kdec_s[h], v_new, (((0,), (0,)), ((), ())),
                                                            preferred_element_type=F32)
        z = proj_ref[:, pl.ds(CONV_DIM + h * A_DV, A_DV)]
        o_ref[:, pl.ds(h * A_DV, A_DV)] = _gate_out(o, z, nw_ref[...]).astype(o_ref.dtype)

    xbuf_ref[0:8, :] = xbuf_ref[ROWS:ROWS + 8, :]

    @pl.when(i == pl.num_programs(1) - 1)
    def _():
        s_out_ref[0] = s_ref[...]
        c_out_ref[0] = xbuf_ref[ROWS:ROWS + 8, :]


def _gdn_prompt(proj, obuf, conv_w, alog_row, dtb_row, norm_w, n_batch, lp):
    nblk = lp // ROWS
    vec = lambda n: pl.BlockSpec((1, n), lambda b, i: (0, 0))
    return pl.pallas_call(
        _gdn_prompt_kernel,
        grid=(n_batch, nblk),
        in_specs=[pl.BlockSpec((ROWS, A_PROJ_PAD), lambda b, i: (b * nblk + i, 0)),
                  pl.BlockSpec((CONV_W, CONV_DIM), lambda b, i: (0, 0)),
                  vec(128), vec(128), vec(A_DV),
                  pl.BlockSpec(memory_space=pl.ANY)],
        out_specs=[pl.BlockSpec((ROWS, D_MODEL), lambda b, i: (b * nblk + i, 0)),
                   pl.BlockSpec((1, A_HEADS, A_DK, A_DV), lambda b, i: (b, 0, 0, 0)),
                   pl.BlockSpec((1, 8, CONV_DIM), lambda b, i: (b, 0, 0))],
        out_shape=[jax.ShapeDtypeStruct(obuf.shape, obuf.dtype),
                   jax.ShapeDtypeStruct((n_batch, A_HEADS, A_DK, A_DV), F32),
                   jax.ShapeDtypeStruct((n_batch, 8, CONV_DIM), F32)],
        scratch_shapes=[pltpu.VMEM((ROWS + 8, CONV_DIM), F32), pltpu.VMEM((A_HEADS, A_DK, A_DV), F32)],
        input_output_aliases={5: 0},
        compiler_params=_cparams("arbitrary", "arbitrary"),
        name="gdn_prompt",
    )(proj, conv_w, alog_row, dtb_row, norm_w.reshape(1, A_DV), obuf)


def _gdn_sample_kernel(proj_ref, c0_ref, s0_ref, cw_ref, alog_ref, dtb_ref, nw_ref, obuf_ref, o_ref, s_out_ref,
                       c_out_ref, xbuf_ref):
    del obuf_ref
    nb, steps = SAMPLE_BB, CHUNK // SAMPLE_BB
    for b in range(nb):
        xbuf_ref[pl.ds(b * 16 + 5, 3), :] = c0_ref[b]
        xbuf_ref[pl.ds(b * 16 + 8, steps), :] = proj_ref[pl.ds(b * steps, steps), 0:CONV_DIM]
        c_out_ref[b] = proj_ref[pl.ds((b + 1) * steps - (CONV_W - 1), CONV_W - 1), 0:CONV_DIM]

    beta, g = _gdn_gates(proj_ref[:, A_MAIN:A_PROJ_PAD], alog_ref[...], dtb_ref[...], None)
    gcum = _group_cumsum(g, steps)
    gcum_t = gcum.T
    gend = _group_last(gcum, steps)

    ri = lax.broadcasted_iota(jnp.int32, (CHUNK, CHUNK), 0)
    ci = lax.broadcasted_iota(jnp.int32, (CHUNK, CHUNK), 1)
    same = (ri // steps) == (ci // steps)
    incl = same & (ri >= ci)
    strict = same & (ri > ci)

    for h in range(A_HEADS):
        qs, ks, vs = [], [], []
        for b in range(nb):
            qb, kb_, vb_ = _conv_head_inputs(xbuf_ref, cw_ref, b * 16 + 8, steps, h)
            qs.append(qb), ks.append(kb_), vs.append(vb_)
        q, k, v = (jnp.concatenate(t, axis=0) for t in (qs, ks, vs))
        g_c = gcum[:, 8 + h:9 + h]
        g_r = gcum_t[8 + h:9 + h, :]
        g_e = gend[:, 8 + h:9 + h]
        u, w, qk, qg, kdec = _wy_block(q, k, v, beta[:, h:h + 1], g_c, g_r, g_e, incl, strict, 3)
        wqg = [jnp.concatenate([w[b * steps:(b + 1) * steps], qg[b * steps:(b + 1) * steps]], axis=0) for b in range(nb)]
        ws_s = [_bdot(wqg[b], s0_ref[b, h]) for b in range(nb)]
        v_news = [u[b * steps:(b + 1) * steps] - ws_s[b][:steps] for b in range(nb)]
        upd = [_bdot_tn(kdec[b * steps:(b + 1) * steps], v_news[b]) for b in range(nb)]
        for b in range(nb):
            s_out_ref[b, h] = s0_ref[b, h] * jnp.exp(g_e[(b + 1) * steps - 1:(b + 1) * steps, :]) + upd[b]
        o = jnp.concatenate([ws_s[b][steps:] for b in range(nb)], axis=0) + _bdot(qk, jnp.concatenate(v_news, axis=0))
        z = proj_ref[:, pl.ds(CONV_DIM + h * A_DV, A_DV)]
        o_ref[:, pl.ds(h * A_DV, A_DV)] = _gate_out(o, z, nw_ref[...]).astype(o_ref.dtype)


def _gdn_sample(proj, obuf, conv0, s0, conv_w, alog_row, dtb_row, norm_w, row0, n_batch):
    blk0 = row0 // CHUNK
    vec = lambda n: pl.BlockSpec((1, n), lambda i: (0, 0))
    st = pl.BlockSpec((SAMPLE_BB, A_HEADS, A_DK, A_DV), lambda i: (i, 0, 0, 0))
    cs = pl.BlockSpec((SAMPLE_BB, CONV_W - 1, CONV_DIM), lambda i: (i, 0, 0))
    return pl.pallas_call(
        _gdn_sample_kernel,
        grid=(n_batch // SAMPLE_BB,),
        in_specs=[pl.BlockSpec((CHUNK, A_PROJ_PAD), lambda i: (blk0 + i, 0)),
                  cs, st,
                  pl.BlockSpec((CONV_W, CONV_DIM), lambda i: (0, 0)),
                  vec(128), vec(128), vec(A_DV),
                  pl.BlockSpec(memory_space=pl.ANY)],
        out_specs=[pl.BlockSpec((CHUNK, D_MODEL), lambda i: (blk0 + i, 0)), st, cs],
        out_shape=[jax.ShapeDtypeStruct(obuf.shape, obuf.dtype),
                   jax.ShapeDtypeStruct(s0.shape, F32),
                   jax.ShapeDtypeStruct(conv0.shape, F32)],
        scratch_shapes=[pltpu.VMEM((SAMPLE_BB * 16, CONV_DIM), F32)],
        input_output_aliases={7: 0},
        compiler_params=_cparams("arbitrary"),
        name="gdn_sample",
    )(proj, conv0, s0, conv_w, alog_row, dtb_row, norm_w.reshape(1, A_DV), obuf)


def _alibi_slope(h):
    return 2.0 ** (-8.0 * (h + 1) / B_HEADS)


def _sink_softmax_pv(pieces, sink):
    m = sink
    for s, _ in pieces:
        m = jnp.maximum(m, jnp.max(s, axis=-1, keepdims=True))
    den = jnp.exp(sink - m)
    acc = None
    ps = []
    for s, _ in pieces:
        p = jnp.exp(s - m)
        den = den + jnp.sum(p, axis=-1, keepdims=True)
        ps.append(p)
    inv = 1.0 / den
    for p, (_, v) in zip(ps, pieces):
        t = _bdot(p * inv, v)
        acc = t if acc is None else acc + t
    return acc


def _attn_prompt_kernel(sink_ref, q_ref, kvp_ref, kvc_ref, obuf_ref, o_ref):
    del obuf_ref
    i = pl.program_id(1)
    r = lax.broadcasted_iota(jnp.int32, (ROWS, 2 * ROWS), 0)
    c = lax.broadcasted_iota(jnp.int32, (ROWS, 2 * ROWS), 1)
    dist = r - c + ROWS
    kpos = i * ROWS - ROWS - LEAD + c
    valid = (kpos >= 0) & (dist >= 0) & (dist <= WINDOW)
    dist_f = dist.astype(F32)
    for g in range(B_KV_HEADS):
        k = jnp.concatenate([kvp_ref[:, pl.ds(g * B_HD, B_HD)], kvc_ref[:, pl.ds(g * B_HD, B_HD)]], axis=0)
        v = jnp.concatenate([kvp_ref[:, pl.ds(B_KV + g * B_HD, B_HD)], kvc_ref[:, pl.ds(B_KV + g * B_HD, B_HD)]], axis=0)
        kb, vb = k.astype(BF16), v.astype(BF16)
        for j in range(B_GROUP):
            h = g * B_GROUP + j
            s = _bdot_nt(q_ref[:, pl.ds(h * B_HD, B_HD)], kb) * (B_HD ** -0.5)
            s = jnp.where(valid, s - _alibi_slope(h) * dist_f, NEG_INF)
            o = _sink_softmax_pv([(s, vb)], sink_ref[h])
            o_ref[:, pl.ds(h * B_HD, B_HD)] = o.astype(o_ref.dtype)


def _attn_prompt(q, kv, obuf, sinks, n_batch, lp):
    nblk = lp // ROWS
    return pl.pallas_call(
        _attn_prompt_kernel,
        grid_spec=pltpu.PrefetchScalarGridSpec(
            num_scalar_prefetch=1,
            grid=(n_batch, nblk),
            in_specs=[pl.BlockSpec((ROWS, D_MODEL), lambda b, i, s: (b * nblk + i, 0)),
                      pl.BlockSpec((ROWS, 2 * B_KV), lambda b, i, s: (b * nblk + jnp.maximum(i - 1, 0), 0)),
                      pl.BlockSpec((ROWS, 2 * B_KV), lambda b, i, s: (b * nblk + i, 0)),
                      pl.BlockSpec(memory_space=pl.ANY)],
            out_specs=pl.BlockSpec((ROWS, D_MODEL), lambda b, i, s: (b * nblk + i, 0)),
        ),
        out_shape=jax.ShapeDtypeStruct(obuf.shape, obuf.dtype),
        input_output_aliases={4: 0},
        compiler_params=_cparams("arbitrary", "arbitrary"),
        name="attn_prompt",
    )(sinks, q, kv, kv, obuf)


ATT_BB = 8
ATT_S = 8


def _attn_sample_kernel(sink_ref, q_ref, kvn_ref, ck_ref, cv_ref, obuf_ref, o_ref):
    del obuf_ref
    bt = ATT_BB * ATT_S
    rows = B_GROUP * bt
    r = lax.broadcasted_iota(jnp.int32, (rows, ATT_BB * WINDOW), 0)
    c = lax.broadcasted_iota(jnp.int32, (rows, ATT_BB * WINDOW), 1)
    dist_c = WINDOW + r % ATT_S - c % WINDOW
    valid_c = ((r % bt) // ATT_S == c // WINDOW) & (dist_c <= WINDOW)
    dist_cf = dist_c.astype(F32)
    rn = lax.broadcasted_iota(jnp.int32, (rows, bt), 0)
    cn = lax.broadcasted_iota(jnp.int32, (rows, bt), 1)
    dist_n = rn % ATT_S - cn % ATT_S
    valid_n = ((rn % bt) // ATT_S == cn // ATT_S) & (dist_n >= 0)
    dist_nf = dist_n.astype(F32)
    hrow = lax.broadcasted_iota(jnp.int32, (rows, 1), 0) // bt
    for g in range(B_KV_HEADS):
        slope = jnp.zeros((rows, 1), F32)
        sink = jnp.zeros((rows, 1), F32)
        for j in range(B_GROUP):
            slope = jnp.where(hrow == j, _alibi_slope(g * B_GROUP + j), slope)
            sink = jnp.where(hrow == j, sink_ref[g * B_GROUP + j], sink)
        q = jnp.concatenate([q_ref[:, pl.ds((g * B_GROUP + j) * B_HD, B_HD)] for j in range(B_GROUP)], axis=0)
        kc = jnp.concatenate([ck_ref[b, :, pl.ds(g * B_HD, B_HD)] for b in range(ATT_BB)], axis=0)
        vc = jnp.concatenate([cv_ref[b, :, pl.ds(g * B_HD, B_HD)] for b in range(ATT_BB)], axis=0)
        kn = kvn_ref[:, pl.ds(g * B_HD, B_HD)]
        vn = kvn_ref[:, pl.ds(B_KV + g * B_HD, B_HD)]
        s_c = _bdot_nt(q, kc) * (B_HD ** -0.5)
        s_c = jnp.where(valid_c, s_c - slope * dist_cf, NEG_INF)
        s_n = _bdot_nt(q, kn) * (B_HD ** -0.5)
        s_n = jnp.where(valid_n, s_n - slope * dist_nf, NEG_INF)
        o = _sink_softmax_pv([(s_c, vc), (s_n, vn)], sink)
        for j in range(B_GROUP):
            h = g * B_GROUP + j
            o_ref[:, pl.ds(h * B_HD, B_HD)] = o[j * bt:(j + 1) * bt].astype(o_ref.dtype)


def _attn_sample(q, kv, obuf, cache_k, cache_v, sinks, row0, n_batch):
    rows = ATT_BB * ATT_S
    blk0 = row0 // rows
    cache = pl.BlockSpec((ATT_BB, WINDOW, B_KV), lambda i, s: (i, 0, 0))
    return pl.pallas_call(
        _attn_sample_kernel,
        grid_spec=pltpu.PrefetchScalarGridSpec(
            num_scalar_prefetch=1,
            grid=(n_batch // ATT_BB,),
            in_specs=[pl.BlockSpec((rows, D_MODEL), lambda i, s: (blk0 + i, 0)),
                      pl.BlockSpec((rows, 2 * B_KV), lambda i, s: (blk0 + i, 0)),
                      cache, cache,
                      pl.BlockSpec(memory_space=pl.ANY)],
            out_specs=pl.BlockSpec((rows, D_MODEL), lambda i, s: (blk0 + i, 0)),
        ),
        out_shape=jax.ShapeDtypeStruct(obuf.shape, obuf.dtype),
        input_output_aliases={5: 0},
        compiler_params=_cparams("arbitrary"),
        name="attn_sample",
    )(sinks, q, kv, cache_k, cache_v, obuf)


def _split2(x):
    hi = x.astype(BF16)
    lo = (x - hi.astype(F32)).astype(BF16)
    return hi, lo


def _router_kernel(x_ref, w_ref, b_ref, idx_ref, gate_ref, rank_ref, cnt_ref, run_ref):
    @pl.when(pl.program_id(0) == 0)
    def _():
        run_ref[...] = jnp.zeros(run_ref.shape, F32)

    xs = _split2(x_ref[...])
    ws = _split2(w_ref[...])
    logits = b_ref[...]
    for i, j in ((1, 1), (0, 1), (1, 0), (0, 0)):
        logits = logits + jnp.dot(xs[i], ws[j], preferred_element_type=F32)
    lane = lax.broadcasted_iota(jnp.int32, logits.shape, 1)
    real = lane < N_EXPERTS
    logits = jnp.where(real, logits, NEG_INF)
    m = jnp.max(logits, axis=-1, keepdims=True)
    e = jnp.where(real, jnp.exp(logits - m), 0.0)
    probs = e / jnp.sum(e, axis=-1, keepdims=True)
    p1 = jnp.max(probs, axis=-1, keepdims=True)
    i1 = jnp.min(jnp.where(probs == p1, lane, 128), axis=-1, keepdims=True)
    rest = jnp.where((lane == i1) | (~real), -1.0, probs)
    p2 = jnp.max(rest, axis=-1, keepdims=True)
    i2 = jnp.min(jnp.where(rest == p2, lane, 128), axis=-1, keepdims=True)
    tot = p1 + p2
    idx_ref[...] = jnp.where(lane == 0, i1, jnp.where(lane == 1, i2, 0))
    gate_ref[...] = jnp.where(lane == 0, p1 / tot, jnp.where(lane == 1, p2 / tot, 0.0))

    tm = logits.shape[0]
    chosen = ((lane == i1) | (lane == i2)).astype(F32)
    earlier = (lax.broadcasted_iota(jnp.int32, (tm, tm), 0) > lax.broadcasted_iota(jnp.int32, (tm, tm), 1))
    before = jnp.dot(earlier.astype(BF16), chosen.astype(BF16), preferred_element_type=F32) + run_ref[...]
    r1 = jnp.sum(jnp.where(lane == i1, before, 0.0), axis=-1, keepdims=True)
    r2 = jnp.sum(jnp.where(lane == i2, before, 0.0), axis=-1, keepdims=True)
    rank_ref[...] = jnp.where(lane == 0, r1, jnp.where(lane == 1, r2, 0.0)).astype(jnp.int32)
    run_ref[...] = run_ref[...] + jnp.sum(chosen, axis=0, keepdims=True)
    cnt_ref[...] = run_ref[...]


def _router(x, w_router, b_router, tm):
    m, d = x.shape
    w = jnp.zeros((d, 128), F32).at[:, :N_EXPERTS].set(w_router)
    b = jnp.zeros((1, 128), F32).at[0, :N_EXPERTS].set(b_router)
    out = pl.BlockSpec((tm, 128), lambda i: (i, 0))
    one = pl.BlockSpec((1, 128), lambda i: (0, 0))
    return pl.pallas_call(
        _router_kernel,
        grid=(m // tm,),
        in_specs=[pl.BlockSpec((tm, d), lambda i: (i, 0)), pl.BlockSpec((d, 128), lambda i: (0, 0)), one],
        out_specs=[out, out, out, one],
        out_shape=[jax.ShapeDtypeStruct((m, 128), jnp.int32), jax.ShapeDtypeStruct((m, 128), F32),
                   jax.ShapeDtypeStruct((m, 128), jnp.int32), jax.ShapeDtypeStruct((1, 128), F32)],
        scratch_shapes=[pltpu.VMEM((1, 128), F32)],
        compiler_params=_cparams("arbitrary"),
        name="moe_router",
    )(x, w, b)


def _moe_ffn_kernel(te_ref, xs_ref, wgu_ref, wd_ref, o_ref, wgub_ref, wdb_ref):
    i = pl.program_id(0)

    @pl.when((i == 0) | (te_ref[i] != te_ref[jnp.maximum(i - 1, 0)]))
    def _():
        wgub_ref[...] = wgu_ref[0].astype(BF16)
        wdb_ref[...] = wd_ref[0].astype(BF16)

    d_e = wd_ref.shape[1]
    gu = jnp.dot(xs_ref[...].astype(BF16), wgub_ref[...], preferred_element_type=F32)
    h = (_silu(gu[:, :d_e]) * gu[:, d_e:]).astype(BF16)
    o_ref[...] = jnp.dot(h, wdb_ref[...], preferred_element_type=F32)


def _moe_ffn(tile_expert, xs, w_gu, w_down):
    a_pad, d = xs.shape
    d_e = w_down.shape[1]
    return pl.pallas_call(
        _moe_ffn_kernel,
        grid_spec=pltpu.PrefetchScalarGridSpec(
            num_scalar_prefetch=1,
            grid=(a_pad // MOE_TM,),
            in_specs=[pl.BlockSpec((MOE_TM, d), lambda i, te: (i, 0)),
                      pl.BlockSpec((1, d, 2 * d_e), lambda i, te: (te[i], 0, 0)),
                      pl.BlockSpec((1, d_e, d), lambda i, te: (te[i], 0, 0))],
            out_specs=pl.BlockSpec((MOE_TM, d), lambda i, te: (i, 0)),
            scratch_shapes=[pltpu.VMEM((d, 2 * d_e), BF16), pltpu.VMEM((d_e, d), BF16)],
        ),
        out_shape=jax.ShapeDtypeStruct((a_pad, d), F32),
        compiler_params=_cparams("arbitrary"),
        name="moe_experts",
    )(tile_expert, xs, w_gu, w_down)


def _moe_layer(x, xb, w_router, b_router, w_gu, w_down, layer, ln_g, ln_b, tm):
    tt = x.shape[0]
    idx, gate, rank, cnt = _router(x, w_router, b_router, tm)
    counts = cnt[0, :N_EXPERTS].astype(jnp.int32)
    padded = ((counts + MOE_TM - 1) // MOE_TM) * MOE_TM
    pend = jnp.cumsum(padded)
    pstart = pend - padded
    dest_l = rank
    for e in range(N_EXPERTS):
        dest_l = dest_l + jnp.where(idx == e, pstart[e], 0)
    dest = dest_l[:, :2].reshape(-1)
    a_pad = 2 * tt + N_EXPERTS * MOE_TM
    src_tok = jnp.zeros((a_pad,), jnp.int32).at[dest].set(jnp.arange(2 * tt, dtype=jnp.int32) // 2)
    tile_start = jnp.arange(a_pad // MOE_TM, dtype=jnp.int32) * MOE_TM
    tile_expert = jnp.minimum(jnp.sum(tile_start[:, None] >= pend[None, :], axis=1), N_EXPERTS - 1).astype(jnp.int32)
    del xb
    xs = jnp.take(x, src_tok, axis=0, mode="clip")
    ys = _moe_ffn(tile_expert + layer * N_EXPERTS, xs, w_gu, w_down)
    dest2 = dest.reshape(tt, 2)
    y0 = jnp.take(ys, dest2[:, 0], axis=0, mode="clip")
    y1 = jnp.take(ys, dest2[:, 1], axis=0, mode="clip")
    return _add2_res_ln(y0, y1, gate, x, ln_g, ln_b, tm)


def kernel(x_prompt, x_sample, state_gdn, state_conv, cache_win_k, cache_win_v, meta_tokens, a_w_in, a_conv_w, a_A_log, a_dt_bias, a_norm_w, a_w_out, b_w_kv, b_w_q, b_sinks, b_w_o, ln_mix_g, ln_mix_b, ln_ffn_g, ln_ffn_b, ffd_w_gu, ffd_w_down, moe_w_router, moe_b_router, moe_w_gu, moe_w_down):
    nb, seq, d = x_prompt.shape
    nsb, steps, _ = x_sample.shape
    lp = LEAD + N_META + seq
    n_prompt = nb * lp
    tt = n_prompt + nsb * steps
    tm = 384 if tt % 384 == 0 else 128
    d_ff = ffd_w_down.shape[1]

    pieces = []
    for b in range(nb):
        pieces += [jnp.zeros((LEAD, d), F32), meta_tokens.astype(F32), x_prompt[b]]
    x = jnp.concatenate(pieces + [x_sample.reshape(nsb * steps, d)], axis=0)
    xb = x.astype(BF16)

    p_gdn, p_conv, s_gdn, s_conv = [], [], [], []
    kv = None
    n_moe, n_exp = moe_w_gu.shape[:2]
    moe_gu_all = moe_w_gu.reshape(n_moe * n_exp, *moe_w_gu.shape[2:])
    moe_down_all = moe_w_down.reshape(n_moe * n_exp, *moe_w_down.shape[2:])
    mix_in = jnp.zeros((tt, d), BF16)
    for l in range(DEPTH):
        if l < N_A_LAYERS:
            w_in = jnp.concatenate([a_w_in[l], jnp.zeros((d, A_PROJ_PAD - a_w_in.shape[2]), F32)], axis=1)
            proj = _matmul(xb, w_in, F32, tm, A_PROJ_PAD // 3)
            alog_row = jnp.zeros((1, 128), F32).at[0, 8:16].set(a_A_log[l])
            dtb_row = jnp.zeros((1, 128), F32).at[0, 8:16].set(a_dt_bias[l])
            mix_in, s_p, c_p = _gdn_prompt(proj, mix_in, a_conv_w[l], alog_row, dtb_row, a_norm_w[l], nb, lp)
            mix_in, s_s, c_s = _gdn_sample(proj, mix_in, state_conv[l], state_gdn[l], a_conv_w[l], alog_row, dtb_row,
                                           a_norm_w[l], n_prompt, nsb)
            p_gdn.append(s_p)
            s_gdn.append(s_s)
            p_conv.append(c_p[:, 8 - (CONV_W - 1):])
            s_conv.append(c_s)
            w_mix = a_w_out[l]
        else:
            j = l - N_A_LAYERS
            if j == 0:
                kv = _matmul(xb, b_w_kv, F32, tm, 2 * B_KV)
            q = _matmul(xb, b_w_q[j], BF16, tm, d)
            mix_in = _attn_prompt(q, kv, mix_in, b_sinks[j], nb, lp)
            mix_in = _attn_sample(q, kv, mix_in, cache_win_k.reshape(nsb, WINDOW, B_KV),
                                  cache_win_v.reshape(nsb, WINDOW, B_KV), b_sinks[j], n_prompt, nsb)
            w_mix = b_w_o[j]
        x, xb = _matmul_res_ln(mix_in, w_mix, x, ln_mix_g[l], ln_mix_b[l], tm)
        if l % 2 == 0:
            hmid = _matmul_swiglu(xb, ffd_w_gu[l // 2], d_ff, tm, d_ff // 2)
            x, xb = _matmul_res_ln(hmid, ffd_w_down[l // 2], x, ln_ffn_g[l], ln_ffn_b[l], tm)
        else:
            x, xb = _moe_layer(x, xb, moe_w_router[l // 2], moe_b_router[l // 2], moe_gu_all, moe_down_all, l // 2,
                               ln_ffn_g[l], ln_ffn_b[l], tm)

    y_prompt = x[:n_prompt].reshape(nb, lp, d)[:, LEAD + N_META:]
    y_sample = x[n_prompt:].reshape(nsb, steps, d)
    kvp = jnp.stack([kv[(b + 1) * lp - WINDOW:(b + 1) * lp] for b in range(nb)]).reshape(nb, WINDOW, 2, B_KV_HEADS, B_HD)
    kvs = kv[n_prompt:].reshape(nsb, steps, 2, B_KV_HEADS, B_HD)
    s_wk = jnp.concatenate([cache_win_k[:, steps:], kvs[:, :, 0]], axis=1)
    s_wv = jnp.concatenate([cache_win_v[:, steps:], kvs[:, :, 1]], axis=1)
    return (y_prompt, y_sample, jnp.stack(p_gdn), jnp.stack(p_conv), kvp[:, :, 0], kvp[:, :, 1],
            jnp.stack(s_gdn), jnp.stack(s_conv), s_wk, s_wv)
```
